```python
import math
import jax, jax.numpy as jnp
from jax import lax
import numpy as np

D_MODEL = 1024
BATCH = 16
SEQ = 2048
DEPTH = 2

HG_HEADS = 4
HG_DIM = 64
HG_WIDTH = HG_HEADS * HG_DIM
HG_CHUNK = 64

ATT_HEADS = 4
ATT_DIM = 64
ATT_WIDTH = ATT_HEADS * ATT_DIM
IDX_HEADS = 8
IDX_DIM = 64
TOPK_MAX = 256
Q_BLOCK = 128
ROPE_THETA = 10000.0

SSM_HEADS = 8
SSM_HEADDIM = 64
SSM_WIDTH = SSM_HEADS * SSM_HEADDIM
SSM_GROUPS = 2
SSM_STATE = 128
SSM_CONV = 4
SSM_CHUNK = 128
CONV_CH = SSM_WIDTH + 2 * SSM_GROUPS * SSM_STATE

MIX_WIDTH = HG_WIDTH + ATT_WIDTH + SSM_WIDTH
IN_SPLITS = (HG_WIDTH, HG_WIDTH, HG_WIDTH, HG_WIDTH,
             ATT_WIDTH, ATT_DIM, ATT_DIM, IDX_HEADS * IDX_DIM, IDX_DIM, IDX_HEADS,
             SSM_WIDTH, CONV_CH, SSM_HEADS)
IN_WIDTH = (4 * HG_WIDTH + ATT_WIDTH + 2 * ATT_DIM + IDX_HEADS * IDX_DIM + IDX_DIM + IDX_HEADS
            + SSM_WIDTH + CONV_CH + SSM_HEADS)

D_FF = 2816
EPS = 1e-6
NEG_BIG = -1e30

kernel_name = "hybrid_hgrn2_dsa_ssd_macaron_adaln"


def rms_norm(x, g):
    xf = x.astype(jnp.float32)
    y = xf * lax.rsqrt(jnp.mean(xf * xf, axis=-1, keepdims=True) + EPS)
    return (y * g.astype(jnp.float32)).astype(x.dtype)


def layer_norm(x, g, b):
    xf = x.astype(jnp.float32)
    xc = xf - jnp.mean(xf, axis=-1, keepdims=True)
    y = xc * lax.rsqrt(jnp.mean(xc * xc, axis=-1, keepdims=True) + EPS)
    return (y * g.astype(jnp.float32) + b.astype(jnp.float32)).astype(x.dtype)


def modulate(h, shift, scale):
    return h * (1 + scale[:, None, :]) + shift[:, None, :]


def swiglu(h, w_gu, w_down):
    gate, up = jnp.split(h @ w_gu, 2, axis=-1)
    return (jax.nn.silu(gate) * up) @ w_down


def rope_tables(seq, dim):
    inv_freq = 1.0 / (ROPE_THETA ** (jnp.arange(0, dim, 2, dtype=jnp.float32) / dim))
    ang = jnp.arange(seq, dtype=jnp.float32)[:, None] * inv_freq[None, :]
    return jnp.cos(ang), jnp.sin(ang)


def apply_rope(x, cos, sin):
    half = x.shape[-1] // 2
    bshape = (x.shape[1],) + (1,) * (x.ndim - 3) + (half,)
    cos = cos.reshape(bshape)
    sin = sin.reshape(bshape)
    xf = x.astype(jnp.float32)
    x1, x2 = xf[..., :half], xf[..., half:]
    return jnp.concatenate([x1 * cos - x2 * sin, x2 * cos + x1 * sin], axis=-1).astype(x.dtype)


def segsum_exp(cs):
    n = cs.shape[-1]
    mask = jnp.tril(jnp.ones((n, n), dtype=bool))
    return jnp.exp(jnp.where(mask, cs[..., :, None] - cs[..., None, :], NEG_BIG))


def hgrn2_mixer(q, f_logit, i, g, lb, norm_g):
    bsz, seq, _ = q.shape
    nc = seq // HG_CHUNK
    f32 = jnp.float32
    fl = f_logit.astype(f32)
    lb = lb.astype(f32)
    f = lb + (1 - lb) * jax.nn.sigmoid(fl)
    log_f = jnp.log(jnp.maximum(f, 1e-30))
    k = (1 - lb) * jax.nn.sigmoid(-fl)

    def chunks(t):
        return t.astype(f32).reshape(bsz, nc, HG_CHUNK, HG_HEADS, HG_DIM).transpose(1, 0, 3, 2, 4)

    causal = jnp.tril(jnp.ones((HG_CHUNK, HG_CHUNK), dtype=bool))[:, :, None]

    def step(state, inp):
        qc, kc, vc, lfc = inp
        b = jnp.cumsum(lfc, axis=2)
        o_inter = jnp.einsum('bhtk,bhkv->bhtv', qc * jnp.exp(b), state)
        diff = b[:, :, :, None, :] - b[:, :, None, :, :]
        decay = jnp.exp(jnp.where(causal, diff, NEG_BIG))
        scores = jnp.einsum('bhtk,bhtsk,bhsk->bhts', qc, decay, kc)
        o = o_inter + jnp.einsum('bhts,bhsv->bhtv', scores, vc)
        b_end = b[:, :, -1:, :]
        state = (state * jnp.exp(b_end[:, :, 0, :, None])
                 + jnp.einsum('bhsk,bhsv->bhkv', kc * jnp.exp(b_end - b), vc))
        return state, o

    s0 = jnp.zeros((bsz, HG_HEADS, HG_DIM, HG_DIM), f32)
    _, o = lax.scan(step, s0, (chunks(q), chunks(k), chunks(i), chunks(log_f)))
    o = o.transpose(1, 0, 3, 2, 4).reshape(bsz, seq, HG_HEADS, HG_DIM)
    o = rms_norm(o, norm_g.reshape(HG_HEADS, HG_DIM)).reshape(bsz, seq, HG_WIDTH)
    return (o * jax.nn.silu(g.astype(f32))).astype(q.dtype)


def dsa_mixer(q, k, v, q_idx, k_idx, w_idx, kn_g, kn_b):
    bsz, seq, _ = q.shape
    n_keep = min(TOPK_MAX, seq // 4)
    cos_a, sin_a = rope_tables(seq, ATT_DIM)
    cos_i, sin_i = rope_tables(seq, IDX_DIM)
    q = apply_rope(q.reshape(bsz, seq, ATT_HEADS, ATT_DIM), cos_a, sin_a)
    k = apply_rope(k, cos_a, sin_a)
    q_idx = apply_rope(q_idx.reshape(bsz, seq, IDX_HEADS, IDX_DIM), cos_i, sin_i)
    k_idx = apply_rope(layer_norm(k_idx, kn_g, kn_b), cos_i, sin_i)
    w_idx = w_idx * (IDX_HEADS ** -0.5 * IDX_DIM ** -0.5)
    key_pos = jnp.arange(seq)

    def block(j):
        t0 = j * Q_BLOCK
        qb = lax.dynamic_slice_in_dim(q, t0, Q_BLOCK, axis=1)
        qib = lax.dynamic_slice_in_dim(q_idx, t0, Q_BLOCK, axis=1)
        wib = lax.dynamic_slice_in_dim(w_idx, t0, Q_BLOCK, axis=1)
        q_pos = t0 + jnp.arange(Q_BLOCK)
        logits = jax.nn.relu(jnp.einsum('bthd,bsd->bths', qib, k_idx))
        score = jnp.einsum('bths,bth->bts', logits, wib).astype(jnp.float32)
        score = jnp.where((key_pos[None, :] <= q_pos[:, None])[None], score, NEG_BIG)
        _, sel = lax.top_k(score, n_keep)
        k_sel = jax.vmap(lambda kk, ii: kk[ii])(k, sel)
        v_sel = jax.vmap(lambda vv, ii: vv[ii])(v, sel)
        s = jnp.einsum('bthd,btkd->bhtk', qb, k_sel).astype(jnp.float32) * ATT_DIM ** -0.5
        s = jnp.where((sel <= q_pos[None, :, None])[:, None], s, NEG_BIG)
        p = jax.nn.softmax(s, axis=-1).astype(v.dtype)
        return jnp.einsum('bhtk,btkd->bthd', p, v_sel)

    out = lax.map(block, jnp.arange(seq // Q_BLOCK))
    return out.transpose(1, 0, 2, 3, 4).reshape(bsz, seq, ATT_WIDTH)


def mamba2_mixer(z, xbc, dt_raw, conv_w, conv_b, dt_bias, a_log, d_skip, norm_g):
    bsz, seq, _ = z.shape
    nc = seq // SSM_CHUNK
    R = SSM_HEADS // SSM_GROUPS
    f32 = jnp.float32
    xbc = lax.conv_general_dilated(xbc, conv_w[:, None, :].astype(xbc.dtype), (1,), [(SSM_CONV - 1, 0)],
                                   dimension_numbers=('NWC', 'WIO', 'NWC'), feature_group_count=CONV_CH)
    xbc = jax.nn.silu(xbc.astype(f32) + conv_b.astype(f32))
    xs, bm, cm = jnp.split(xbc, [SSM_WIDTH, SSM_WIDTH + SSM_GROUPS * SSM_STATE], axis=-1)
    dt = jax.nn.softplus(dt_raw.astype(f32) + dt_bias.astype(f32))
    a = -jnp.exp(a_log.astype(f32))
    x6 = xs.reshape(bsz, nc, SSM_CHUNK, SSM_GROUPS, R, SSM_HEADDIM)
    dt5 = dt.reshape(bsz, nc, SSM_CHUNK, SSM_GROUPS, R)
    bm = bm.reshape(bsz, nc, SSM_CHUNK, SSM_GROUPS, SSM_STATE)
    cm = cm.reshape(bsz, nc, SSM_CHUNK, SSM_GROUPS, SSM_STATE)
    xdt = x6 * dt5[..., None]
    a_cs = jnp.cumsum((dt5 * a.reshape(SSM_GROUPS, R)).transpose(0, 3, 4, 1, 2), axis=-1)
    decay_in = segsum_exp(a_cs)
    cb = jnp.einsum('bclgn,bcsgn->bgcls', cm, bm)
    y_diag = jnp.einsum('bgrcls,bcsgrp->bclgrp', cb[:, :, None] * decay_in, xdt)
    states = jnp.einsum('bclgn,bgrcl,bclgrp->bcgrpn', bm, jnp.exp(a_cs[..., -1:] - a_cs), xdt)
    chunk_cs = jnp.cumsum(jnp.pad(a_cs[..., -1], ((0, 0), (0, 0), (0, 0), (1, 0))), axis=-1)
    decay_chunk = segsum_exp(chunk_cs)
    states = jnp.concatenate([jnp.zeros_like(states[:, :1]), states], axis=1)
    prev = jnp.einsum('bgrzc,bcgrpn->bzgrpn', decay_chunk, states)[:, :-1]
    y_off = jnp.einsum('bclgn,bcgrpn,bgrcl->bclgrp', cm, prev, jnp.exp(a_cs))
    y = (y_diag + y_off).reshape(bsz, seq, SSM_WIDTH) + xs * jnp.repeat(d_skip.astype(f32), SSM_HEADDIM)
    y = y * jax.nn.silu(z.astype(f32))
    y = rms_norm(y.reshape(bsz, seq, SSM_GROUPS, SSM_WIDTH // SSM_GROUPS),
                 norm_g.reshape(SSM_GROUPS, SSM_WIDTH // SSM_GROUPS))
    return y.reshape(bsz, seq, SSM_WIDTH).astype(z.dtype)


def token_mixing(h, w_in, w_out, lb, hg_norm_g, kn_g, kn_b, conv_w, conv_b, dt_bias, a_log, d_skip, ssm_norm_g):
    offsets = [int(o) for o in np.cumsum(IN_SPLITS)[:-1]]
    (hq, hf, hi, hg, aq, ak, av, iq, ik, iw, sz, sxbc, sdt) = jnp.split(h @ w_in, offsets, axis=-1)
    o_a = hgrn2_mixer(hq, hf, hi, hg, lb, hg_norm_g)
    o_b = dsa_mixer(aq, ak, av, iq, ik, iw, kn_g, kn_b).astype(h.dtype)
    o_c = mamba2_mixer(sz, sxbc, sdt, conv_w, conv_b, dt_bias, a_log, d_skip, ssm_norm_g)
    return jnp.concatenate([o_a, o_b, o_c], axis=-1) @ w_out


def setup_inputs(seed: int = 0) -> dict:
    key = jax.random.key(seed)
    ks = jax.random.split(key, 24)
    f32 = jnp.float32

    def nrm(k, shape, s):
        return jax.random.normal(k, shape, f32) * s

    x = nrm(ks[0], (BATCH, SEQ, D_MODEL), 1.0)
    c = nrm(ks[1], (BATCH, D_MODEL), 1.0)
    w_ada = nrm(ks[2], (DEPTH, D_MODEL, 9 * D_MODEL), 0.5 * D_MODEL ** -0.5)
    b_ada = nrm(ks[3], (DEPTH, 9 * D_MODEL), 0.02)
    norm_g = 1.0 + nrm(ks[4], (DEPTH, 3, D_MODEL), 0.05)
    w_ffn_gu = nrm(ks[5], (DEPTH, 2, D_MODEL, 2 * D_FF), D_MODEL ** -0.5)
    w_ffn_down = nrm(ks[6], (DEPTH, 2, D_FF, D_MODEL), D_FF ** -0.5)
    w_in = nrm(ks[7], (DEPTH, D_MODEL, IN_WIDTH), D_MODEL ** -0.5)
    w_out = nrm(ks[8], (DEPTH, MIX_WIDTH, D_MODEL), MIX_WIDTH ** -0.5)
    lb_logits = nrm(ks[9], (DEPTH, HG_WIDTH), 1.0)
    hg_norm_g = 1.0 + nrm(ks[10], (DEPTH, HG_WIDTH), 0.05)
    idx_k_norm_g = 1.0 + nrm(ks[11], (DEPTH, IDX_DIM), 0.05)
    idx_k_norm_b = nrm(ks[12], (DEPTH, IDX_DIM), 0.01)
    conv_w = nrm(ks[13], (DEPTH, SSM_CONV, CONV_CH), SSM_CONV ** -0.5)
    conv_b = nrm(ks[14], (DEPTH, CONV_CH), 0.01)
    dt0 = jnp.exp(jax.random.uniform(ks[15], (DEPTH, SSM_HEADS), f32, math.log(1e-3), math.log(1e-1)))
    dt_bias = dt0 + jnp.log(-jnp.expm1(-dt0))
    a_log = jnp.log(jax.random.uniform(ks[16], (DEPTH, SSM_HEADS), f32, 1.0, 16.0))
    d_skip = 1.0 + nrm(ks[17], (DEPTH, SSM_HEADS), 0.1)
    ssm_norm_g = 1.0 + nrm(ks[18], (DEPTH, SSM_WIDTH), 0.05)
    final_norm_g = 1.0 + nrm(ks[19], (D_MODEL,), 0.05)
    return {"x": x, "c": c, "w_ada": w_ada, "b_ada": b_ada, "norm_g": norm_g,
            "w_ffn_gu": w_ffn_gu, "w_ffn_down": w_ffn_down, "w_in": w_in, "w_out": w_out,
            "lb_logits": lb_logits, "hg_norm_g": hg_norm_g, "idx_k_norm_g": idx_k_norm_g,
            "idx_k_norm_b": idx_k_norm_b, "conv_w": conv_w, "conv_b": conv_b, "dt_bias": dt_bias,
            "a_log": a_log, "d_skip": d_skip, "ssm_norm_g": ssm_norm_g, "final_norm_g": final_norm_g}


def reference(x, c, w_ada, b_ada, norm_g, w_ffn_gu, w_ffn_down, w_in, w_out, lb_logits, hg_norm_g,
              idx_k_norm_g, idx_k_norm_b, conv_w, conv_b, dt_bias, a_log, d_skip, ssm_norm_g, final_norm_g):
    sm = jax.nn.softmax(lb_logits.astype(jnp.float32), axis=0)
    lower_bounds = jnp.cumsum(sm, axis=0) - sm[:1]
    cond = jax.nn.silu(c)
    for l in range(DEPTH):
        mod = cond @ w_ada[l] + b_ada[l]
        sh1, sc1, g1, sh2, sc2, g2, sh3, sc3, g3 = jnp.split(mod, 9, axis=-1)
        h = modulate(rms_norm(x, norm_g[l, 0]), sh1, sc1)
        x = x + 0.5 * g1[:, None, :] * swiglu(h, w_ffn_gu[l, 0], w_ffn_down[l, 0])
        h = modulate(rms_norm(x, norm_g[l, 1]), sh2, sc2)
        x = x + g2[:, None, :] * token_mixing(h, w_in[l], w_out[l], lower_bounds[l], hg_norm_g[l],
                                              idx_k_norm_g[l], idx_k_norm_b[l], conv_w[l], conv_b[l],
                                              dt_bias[l], a_log[l], d_skip[l], ssm_norm_g[l])
        h = modulate(rms_norm(x, norm_g[l, 2]), sh3, sc3)
        x = x + 0.5 * g3[:, None, :] * swiglu(h, w_ffn_gu[l, 1], w_ffn_down[l, 1])
    return rms_norm(x, final_norm_g)
```

```python
import functools

import numpy as np
import jax
import jax.numpy as jnp
from jax import lax
from jax.experimental import pallas as pl
from jax.experimental.pallas import tpu as pltpu

F32 = jnp.float32
BF16 = jnp.bfloat16
HIGHEST = lax.Precision.HIGHEST

D_MODEL = 1024
DEPTH = 2
HG_HEADS = 4
HG_DIM = 64
HG_WIDTH = HG_HEADS * HG_DIM
HG_CHUNK = 64
HG_SUB = 16
ATT_HEADS = 4
ATT_DIM = 64
ATT_WIDTH = ATT_HEADS * ATT_DIM
IDX_HEADS = 8
IDX_DIM = 64
TOPK_MAX = 256
Q_BLOCK = 128
ROPE_THETA = 10000.0
SSM_HEADS = 8
SSM_HEADDIM = 64
SSM_WIDTH = SSM_HEADS * SSM_HEADDIM
SSM_GROUPS = 2
SSM_STATE = 128
SSM_CONV = 4
SSM_CHUNK = 128
CONV_CH = SSM_WIDTH + 2 * SSM_GROUPS * SSM_STATE
MIX_WIDTH = HG_WIDTH + ATT_WIDTH + SSM_WIDTH
D_FF = 2816
EPS = 1e-6
NEG_BIG = -1e30
INT_MIN = -(2 ** 31)

LANES = 128
VMEM_LIMIT_BYTES = 56 * 1024 * 1024

P_HG = 0
P_IQ = 1024
P_AQ = 1536
P_KV = 1792
P_XBC = 2048
P_Z = 3072
P_DT = 3584
P_TOTAL = 3712

TOK_TILE = 512
FFN_CHUNK = 1408


def _sigmoid(x):
    return 1.0 / (1.0 + jnp.exp(-x))


def _norm_mod(x, g, shift, scale):
    y = x * lax.rsqrt(jnp.mean(x * x, axis=-1, keepdims=True) + EPS)
    return (y * g) * (1.0 + scale) + shift


def _dot(a, b):
    return jnp.dot(a, b, preferred_element_type=F32)


def _dot_nt(a, b):
    return lax.dot_general(a, b, (((1,), (1,)), ((), ())), preferred_element_type=F32)


def _dot_tn(a, b):
    return lax.dot_general(a, b, (((0,), (0,)), ((), ())), preferred_element_type=F32)


def _params(sem):
    return pltpu.CompilerParams(dimension_semantics=sem, vmem_limit_bytes=VMEM_LIMIT_BYTES)


def _const_spec(shape):
    nd = len(shape)
    return pl.BlockSpec(shape, lambda *_: (0,) * nd, pipeline_mode=pl.Buffered(1))


def _ada_kernel(c_ref, w_ref, b_ref, o_ref):
    c = c_ref[...]
    cond = c * _sigmoid(c)
    o_ref[...] = jnp.dot(cond, w_ref[...], preferred_element_type=F32, precision=HIGHEST) + b_ref[...]


def _ada(c, w_ada, b_ada):
    bsz = c.shape[0]
    n_mod = w_ada.shape[-1]
    nt = n_mod // D_MODEL
    return pl.pallas_call(
        _ada_kernel,
        out_shape=jax.ShapeDtypeStruct((DEPTH, bsz, n_mod), F32),
        grid=(DEPTH, nt),
        in_specs=[
            pl.BlockSpec((bsz, D_MODEL), lambda l, n: (0, 0)),
            pl.BlockSpec((None, D_MODEL, D_MODEL), lambda l, n: (l, 0, n)),
            pl.BlockSpec((None, 1, D_MODEL), lambda l, n: (l, 0, n)),
        ],
        out_specs=pl.BlockSpec((None, bsz, D_MODEL), lambda l, n: (l, 0, n)),
        compiler_params=_params(("arbitrary", "arbitrary")),
        name="ada_mod",
    )(c, w_ada, b_ada.reshape(DEPTH, 1, n_mod))


def _ffn_kernel(x_ref, mod_ref, ng_ref, wgu_ref, wd_ref, fg_ref, o_ref, act_ref, *, sub, final):
    x = x_ref[...]
    h = _norm_mod(x, ng_ref[sub:sub + 1, :], mod_ref[3 * sub:3 * sub + 1, :], mod_ref[3 * sub + 1:3 * sub + 2, :])
    h = h.astype(BF16)
    for c0 in range(0, D_FF, FFN_CHUNK):
        gate = _dot(h, wgu_ref[:, c0:c0 + FFN_CHUNK])
        up = _dot(h, wgu_ref[:, D_FF + c0:D_FF + c0 + FFN_CHUNK])
        act_ref[:, c0:c0 + FFN_CHUNK] = (gate * _sigmoid(gate) * up).astype(BF16)
    out = _dot(act_ref[...], wd_ref[...])
    y = x + (0.5 * mod_ref[3 * sub + 2:3 * sub + 3, :]) * out
    if final:
        y = (y * lax.rsqrt(jnp.mean(y * y, axis=-1, keepdims=True) + EPS)) * fg_ref[...]
    o_ref[...] = y


def _ffn(x, mod4, norm_g, wgu, wd, final_g, layer, sub, final):
    bsz, seq, _ = x.shape
    kern = functools.partial(_ffn_kernel, sub=sub, final=final)
    return pl.pallas_call(
        kern,
        out_shape=jax.ShapeDtypeStruct(x.shape, F32),
        grid=(bsz, seq // TOK_TILE),
        in_specs=[
            pl.BlockSpec((None, TOK_TILE, D_MODEL), lambda b, i: (b, i, 0)),
            pl.BlockSpec((None, None, 9, D_MODEL), lambda b, i: (layer, b, 0, 0)),
            pl.BlockSpec((None, 3, D_MODEL), lambda b, i: (layer, 0, 0)),
            _const_spec((D_MODEL, 2 * D_FF)),
            _const_spec((D_FF, D_MODEL)),
            _const_spec((1, D_MODEL)),
        ],
        out_specs=pl.BlockSpec((None, TOK_TILE, D_MODEL), lambda b, i: (b, i, 0)),
        scratch_shapes=[pltpu.VMEM((TOK_TILE, D_FF), BF16)],
        compiler_params=_params(("arbitrary", "arbitrary")),
        name="ffn",
    )(x, mod4, norm_g, wgu, wd, final_g)


def _inproj_kernel(x_ref, mod_ref, ng_ref, w_ref, o_ref):
    h = _norm_mod(x_ref[...], ng_ref[1:2, :], mod_ref[3:4, :], mod_ref[4:5, :]).astype(BF16)
    o_ref[...] = _dot(h, w_ref[...])


def _inproj(x, mod4, norm_g, w_packed, layer):
    bsz, seq, _ = x.shape
    return pl.pallas_call(
        _inproj_kernel,
        out_shape=jax.ShapeDtypeStruct((bsz, seq, P_TOTAL), F32),
        grid=(bsz, seq // TOK_TILE),
        in_specs=[
            pl.BlockSpec((None, TOK_TILE, D_MODEL), lambda b, i: (b, i, 0)),
            pl.BlockSpec((None, None, 9, D_MODEL), lambda b, i: (layer, b, 0, 0)),
            pl.BlockSpec((None, 3, D_MODEL), lambda b, i: (layer, 0, 0)),
            _const_spec((D_MODEL, P_TOTAL)),
        ],
        out_specs=pl.BlockSpec((None, TOK_TILE, P_TOTAL), lambda b, i: (b, i, 0)),
        compiler_params=_params(("arbitrary", "arbitrary")),
        name="in_proj",
    )(x, mod4, norm_g, w_packed)


def _pack_w_in(w):
    z = lambda n: jnp.zeros((w.shape[0], n), w.dtype)
    cols = [
        w[:, 0:1024],
        w[:, 1408:1920],
        w[:, 1024:1280],
        w[:, 1280:1344],
        w[:, 1920:1984],
        w[:, 1344:1408],
        w[:, 1984:1992],
        z(56),
        w[:, 2504:3528],
        w[:, 1992:2504],
        w[:, 3528:3536],
        z(120),
    ]
    return jnp.concatenate(cols, axis=1).astype(BF16)


def _outproj_kernel(x_ref, oa_ref, ob_ref, oc_ref, mod_ref, w_ref, o_ref):
    acc = _dot(oa_ref[...], w_ref[0:HG_WIDTH, :])
    acc += _dot(ob_ref[...], w_ref[HG_WIDTH:HG_WIDTH + ATT_WIDTH, :])
    acc += _dot(oc_ref[...], w_ref[HG_WIDTH + ATT_WIDTH:MIX_WIDTH, :])
    o_ref[...] = x_ref[...] + mod_ref[5:6, :] * acc


def _outproj(x, o_a, o_b, o_c, mod4, w_out, layer):
    bsz, seq, _ = x.shape
    tile = lambda w: pl.BlockSpec((None, TOK_TILE, w), lambda b, i: (b, i, 0))
    return pl.pallas_call(
        _outproj_kernel,
        out_shape=jax.ShapeDtypeStruct(x.shape, F32),
        grid=(bsz, seq // TOK_TILE),
        in_specs=[
            tile(D_MODEL), tile(HG_WIDTH), tile(ATT_WIDTH), tile(SSM_WIDTH),
            pl.BlockSpec((None, None, 9, D_MODEL), lambda b, i: (layer, b, 0, 0)),
            _const_spec((MIX_WIDTH, D_MODEL)),
        ],
        out_specs=tile(D_MODEL),
        compiler_params=_params(("arbitrary", "arbitrary")),
        name="out_proj",
    )(x, o_a, o_b, o_c, mod4, w_out)


def _split3(x):
    hi = x.astype(BF16)
    r = x - hi.astype(F32)
    mid = r.astype(BF16)
    lo = (r - mid.astype(F32)).astype(BF16)
    return hi, mid, lo


def _hgrn2_kernel(blk_ref, lbl_ref, ng_ref, tril_ref, bones_ref, o_ref, st_ref, p_ref, *, layer):
    ch = pl.program_id(1)
    C, W, SB = HG_CHUNK, HG_WIDTH, HG_SUB

    @pl.when(ch == 0)
    def _():
        st_ref[...] = jnp.zeros_like(st_ref)

    lg = lbl_ref[...]
    e = jnp.exp(lg - jnp.max(lg, axis=0, keepdims=True))
    sm = e / jnp.sum(e, axis=0, keepdims=True)
    cs = sm[0:1, :]
    for i in range(1, layer + 1):
        cs = cs + sm[i:i + 1, :]
    lb = cs - sm[0:1, :]

    q = blk_ref[:, 0:W]
    fl = blk_ref[:, W:2 * W]
    iv = blk_ref[:, 2 * W:3 * W]
    g = blk_ref[:, 3 * W:4 * W]
    f = lb + (1.0 - lb) * _sigmoid(fl)
    logf = jnp.log(jnp.maximum(f, 1e-30))
    kk = (1.0 - lb) * _sigmoid(-fl)
    b = jnp.dot(tril_ref[...], logf, preferred_element_type=F32, precision=HIGHEST)

    lane = lax.broadcasted_iota(jnp.int32, (1, W), 1)
    hmask = [(lane >= h * HG_DIM) & (lane < (h + 1) * HG_DIM) for h in range(HG_HEADS)]
    bones = bones_ref[...]
    iv_bf = iv.astype(BF16)

    rmod = lax.broadcasted_iota(jnp.int32, (C, W), 0) & (SB - 1)
    for d in range(SB):
        if d == 0:
            p = q * kk
        else:
            valid = rmod >= d
            dec = jnp.exp(jnp.where(valid, b - pltpu.roll(b, d, 0), 0.0))
            p = jnp.where(valid, q * pltpu.roll(kk, d, 0) * dec, 0.0)
        p_ref[d * C:(d + 1) * C, :] = p.astype(BF16)
    wsum = _dot(p_ref[...], bones)
    o = wsum[0:C, :] * iv
    for d in range(1, SB):
        o = o + wsum[d * C:(d + 1) * C, :] * pltpu.roll(iv, d, 0)

    parts = [jnp.zeros((SB, W), F32)]
    for i_sub in range(1, C // SB):
        r0 = i_sub * SB
        bref = b[r0 - 1:r0, :]
        qp = q[r0:r0 + SB, :] * jnp.exp(b[r0:r0 + SB, :] - bref)
        kp = (kk[0:r0, :] * jnp.exp(bref - b[0:r0, :])).astype(BF16)
        qstk = jnp.concatenate([jnp.where(hmask[h], qp, 0.0) for h in range(HG_HEADS)], axis=0).astype(BF16)
        sc = _dot_nt(qstk, kp)
        oi = _dot(sc.astype(BF16), iv_bf[0:r0, :])
        acc = jnp.where(hmask[0], oi[0:SB, :], 0.0)
        for h in range(1, HG_HEADS):
            acc = acc + jnp.where(hmask[h], oi[h * SB:(h + 1) * SB, :], 0.0)
        parts.append(acc)
    o = o + jnp.concatenate(parts, axis=0)

    st = st_ref[...]
    o = o + _dot_nt((q * jnp.exp(b)).astype(BF16), st.astype(BF16))
    b_end = b[C - 1:C, :]
    kd = (kk * jnp.exp(b_end - b)).astype(BF16)
    upd = _dot_tn(iv_bf, kd)
    st_ref[...] = st * jnp.exp(b_end) + jnp.where(bones > 0, upd, 0.0)

    hi, mid, lo = _split3(o * o)
    ss = _dot(jnp.concatenate([hi, mid, lo], axis=0), bones)
    ms = (ss[0:C, :] + ss[C:2 * C, :] + ss[2 * C:3 * C, :]) * (1.0 / HG_DIM)
    o = (o * lax.rsqrt(ms + EPS)) * ng_ref[...]
    o_ref[...] = (o * (g * _sigmoid(g))).astype(BF16)


def _hgrn2(proj, lb_logits, norm_g, layer):
    bsz, seq, _ = proj.shape
    tril = jnp.tril(jnp.ones((HG_CHUNK, HG_CHUNK), F32))
    head = np.arange(HG_WIDTH) // HG_DIM
    bones = jnp.asarray(head[:, None] == head[None, :], BF16)
    kern = functools.partial(_hgrn2_kernel, layer=layer)
    return pl.pallas_call(
        kern,
        out_shape=jax.ShapeDtypeStruct((bsz, seq, HG_WIDTH), BF16),
        grid=(bsz, seq // HG_CHUNK),
        in_specs=[
            pl.BlockSpec((None, HG_CHUNK, 4 * HG_WIDTH), lambda b, c: (b, c, P_HG // (4 * HG_WIDTH))),
            pl.BlockSpec((DEPTH, HG_WIDTH), lambda b, c: (0, 0)),
            pl.BlockSpec((1, HG_WIDTH), lambda b, c: (0, 0)),
            pl.BlockSpec((HG_CHUNK, HG_CHUNK), lambda b, c: (0, 0)),
            pl.BlockSpec((HG_WIDTH, HG_WIDTH), lambda b, c: (0, 0)),
        ],
        out_specs=pl.BlockSpec((None, HG_CHUNK, HG_WIDTH), lambda b, c: (b, c, 0)),
        scratch_shapes=[
            pltpu.VMEM((HG_WIDTH, HG_WIDTH), F32),
            pltpu.VMEM((HG_SUB * HG_CHUNK, HG_WIDTH), BF16),
        ],
        compiler_params=_params(("arbitrary", "arbitrary")),
        name="hgrn2",
    )(proj, lb_logits, norm_g[layer].reshape(1, HG_WIDTH), tril, bones)


def _dsa_kernel(iq_ref, aq_ref, kv_ref, cos_ref, sin_ref, lng_ref, lnb_ref, o_ref,
                kidx_ref, kr_ref, vt_ref, score_ref, key_ref, bias_ref, thr_ref, sg_ref, *, n_keep, n_blocks):
    j = pl.program_id(1)
    QB = Q_BLOCK
    lane = lax.broadcasted_iota(jnp.int32, (QB, LANES), 1)
    lo64 = lane < 64
    first_half = (lane & 63) < 32
    cos = cos_ref[...]
    sin = sin_ref[...]

    def rope(x):
        rot = jnp.where(first_half, pltpu.roll(x, 96, 1), pltpu.roll(x, 32, 1))
        return x * cos + rot * sin

    def head_pair(x):
        return jnp.concatenate([jnp.where(lo64, x, 0.0), jnp.where(lo64, 0.0, x)], axis=0).astype(BF16)

    @pl.when(j == 0)
    def _():
        kidx_ref[...] = jnp.zeros_like(kidx_ref)
        kr_ref[...] = jnp.zeros_like(kr_ref)
        vt_ref[...] = jnp.zeros_like(vt_ref)

    a = kv_ref[:, 0:LANES]
    mu = jnp.sum(jnp.where(lo64, 0.0, a), axis=-1, keepdims=True) * (1.0 / IDX_DIM)
    xc = a - mu
    var = jnp.sum(jnp.where(lo64, 0.0, xc * xc), axis=-1, keepdims=True) * (1.0 / IDX_DIM)
    y = (xc * lax.rsqrt(var + EPS)) * lng_ref[...] + lnb_ref[...]
    kr = rope(jnp.where(lo64, a, y))
    krs = pltpu.roll(kr, 64, 1)
    kr_ref[j] = jnp.where(lo64, kr, krs).astype(BF16)
    kidx_ref[j] = jnp.where(lo64, krs, kr).astype(BF16)
    bvt = kv_ref[:, LANES:2 * LANES].T
    vt_ref[j] = bvt[0:ATT_DIM, :].astype(BF16)
    w_t = bvt[ATT_DIM:ATT_DIM + IDX_HEADS, :] * (IDX_HEADS ** -0.5 * IDX_DIM ** -0.5)

    def block(nb):
        nk = nb * QB
        sidx = lax.broadcasted_iota(jnp.int32, (nk, QB), 0)
        tpos = j * QB + lax.broadcasted_iota(jnp.int32, (nk, QB), 1)

        kidx = kidx_ref[0:nb].reshape(nk, LANES)
        for m in range(IDX_HEADS // 2):
            qr = rope(iq_ref[:, m * LANES:(m + 1) * LANES])
            l2 = _dot_nt(kidx, head_pair(qr))
            t = (jnp.maximum(l2[:, 0:QB], 0.0) * w_t[2 * m:2 * m + 1, :]
                 + jnp.maximum(l2[:, QB:2 * QB], 0.0) * w_t[2 * m + 1:2 * m + 2, :])
            if m == 0:
                score_ref[0:nk, :] = t
            else:
                score_ref[0:nk, :] += t

        sc = score_ref[0:nk, :]
        sc = jnp.where(sc == 0.0, 0.0, sc)
        bits = lax.bitcast_convert_type(sc, jnp.int32)
        key = bits ^ ((bits >> 31) & jnp.int32(0x7FFFFFFF))
        key_ref[0:nk, :] = jnp.where(sidx <= tpos, key, INT_MIN)

        def count(pred_fn):
            return jnp.sum(jnp.where(pred_fn(key_ref[0:nk, :]), 1.0, 0.0), axis=0, keepdims=True)

        @pl.when(j * QB < n_keep)
        def _():
            thr_ref[...] = jnp.full((1, QB), INT_MIN, jnp.int32)
            sg_ref[...] = jnp.full((1, QB), -1, jnp.int32)

        @pl.when(j * QB >= n_keep)
        def _():
            cnt0 = count(lambda k: k >= 0)
            base = jnp.where(cnt0 >= n_keep, jnp.int32(0), jnp.int32(INT_MIN))

            def body(i, lo):
                cand = lo | lax.shift_left(jnp.int32(1), 30 - i)
                cnt = count(lambda k: k >= cand)
                return jnp.where(cnt >= n_keep, cand, lo)

            thr = lax.fori_loop(0, 31, body, base)
            need = n_keep - count(lambda k: k > thr)
            n_eq = count(lambda k: k == thr)
            thr_ref[...] = thr
            sg_ref[...] = jnp.full((1, QB), nk, jnp.int32)

            @pl.when(jnp.max(n_eq - need) > 0.5)
            def _():
                def body2(i, sg):
                    cand = sg | lax.shift_left(jnp.int32(1), 10 - i)
                    cnt = count(lambda k: (k == thr) & (sidx < cand))
                    return jnp.where(cnt < need, cand, sg)

                sg_ref[...] = lax.fori_loop(0, 11, body2, jnp.zeros((1, QB), jnp.int32))

        key = key_ref[0:nk, :]
        thr = thr_ref[...]
        tie = jnp.where(sidx <= sg_ref[...], jnp.inf, NEG_BIG)
        bias_ref[0:nk, :] = jnp.where(key > thr, jnp.inf, jnp.where(key == thr, tie, NEG_BIG))

        krd = kr_ref[0:nb].reshape(nk, LANES)
        vt = jnp.concatenate([vt_ref[i] for i in range(nb)], axis=1)
        outs = []
        for m in range(ATT_HEADS // 2):
            qr = rope(aq_ref[:, m * LANES:(m + 1) * LANES]) * (ATT_DIM ** -0.5)
            s2 = _dot_nt(krd, head_pair(qr))
            for hh in range(2):
                s = jnp.minimum(s2[:, hh * QB:(hh + 1) * QB], bias_ref[0:nk, :])
                mx = jnp.max(s, axis=0, keepdims=True)
                p = jnp.exp(s - mx)
                l = jnp.sum(p, axis=0, keepdims=True)
                o_t = _dot(vt, p.astype(BF16))
                outs.append(o_t * (1.0 / l))
        o_ref[...] = jnp.concatenate(outs, axis=0).T.astype(BF16)

    per_class = n_blocks // 4
    for cls in range(4):
        @pl.when((j >= cls * per_class) & (j < (cls + 1) * per_class))
        def _(cls=cls):
            block((cls + 1) * per_class)


def _rope_tables(seq):
    inv_freq = 1.0 / (ROPE_THETA ** (jnp.arange(0, ATT_DIM, 2, dtype=F32) / ATT_DIM))
    ang = jnp.arange(seq, dtype=F32)[:, None] * inv_freq[None, :]
    cos, sin = jnp.cos(ang), jnp.sin(ang)
    return jnp.concatenate([cos] * 4, axis=1), jnp.concatenate([-sin, sin, -sin, sin], axis=1)


def _dsa(proj, cos, sin, kn_g, kn_b, layer):
    bsz, seq, _ = proj.shape
    n_blocks = seq // Q_BLOCK
    n_keep = min(TOPK_MAX, seq // 4)
    pad = jnp.zeros((IDX_DIM,), F32)
    lng = jnp.concatenate([pad, kn_g[layer]]).reshape(1, LANES)
    lnb = jnp.concatenate([pad, kn_b[layer]]).reshape(1, LANES)
    kern = functools.partial(_dsa_kernel, n_keep=n_keep, n_blocks=n_blocks)
    col = lambda width, off: pl.BlockSpec((None, Q_BLOCK, width), lambda b, j: (b, j, off // width))
    return pl.pallas_call(
        kern,
        out_shape=jax.ShapeDtypeStruct((bsz, seq, ATT_WIDTH), BF16),
        grid=(bsz, n_blocks),
        in_specs=[
            col(IDX_HEADS * IDX_DIM, P_IQ), col(ATT_WIDTH, P_AQ), col(2 * LANES, P_KV),
            pl.BlockSpec((Q_BLOCK, LANES), lambda b, j: (j, 0)),
            pl.BlockSpec((Q_BLOCK, LANES), lambda b, j: (j, 0)),
            pl.BlockSpec((1, LANES), lambda b, j: (0, 0)),
            pl.BlockSpec((1, LANES), lambda b, j: (0, 0)),
        ],
        out_specs=pl.BlockSpec((None, Q_BLOCK, ATT_WIDTH), lambda b, j: (b, j, 0)),
        scratch_shapes=[
            pltpu.VMEM((n_blocks, Q_BLOCK, LANES), BF16),
            pltpu.VMEM((n_blocks, Q_BLOCK, LANES), BF16),
            pltpu.VMEM((n_blocks, ATT_DIM, Q_BLOCK), BF16),
            pltpu.VMEM((seq, Q_BLOCK), F32),
            pltpu.VMEM((seq, Q_BLOCK), jnp.int32),
            pltpu.VMEM((seq, Q_BLOCK), F32),
            pltpu.VMEM((1, Q_BLOCK), jnp.int32),
            pltpu.VMEM((1, Q_BLOCK), jnp.int32),
        ],
        compiler_params=_params(("arbitrary", "arbitrary")),
        name="dsa",
    )(proj, proj, proj, cos, sin, lng, lnb)


def _ssd_kernel(xbc_ref, z_ref, dt_ref, cw_ref, cb_ref, dtb_ref, alog_ref, dsk_ref, ng_ref, tril_ref,
                o_ref, xpad_ref, st_ref):
    ch = pl.program_id(1)
    L = SSM_CHUNK
    HW = SSM_WIDTH // SSM_GROUPS

    @pl.when(ch == 0)
    def _():
        xpad_ref[0:8, :] = jnp.zeros((8, CONV_CH), F32)
        st_ref[...] = jnp.zeros_like(st_ref)

    xpad_ref[8:8 + L, :] = xbc_ref[...]
    conv = cb_ref[...]
    for i in range(SSM_CONV):
        off = 8 - (SSM_CONV - 1) + i
        conv = conv + cw_ref[i:i + 1, :] * xpad_ref[off:off + L, :]
    xpad_ref[0:8, :] = xpad_ref[L:L + 8, :]
    xc = conv * _sigmoid(conv)
    xs = xc[:, 0:SSM_WIDTH]

    dtr = dt_ref[...] + dtb_ref[...]
    dt128 = jnp.maximum(dtr, 0.0) + jnp.log(1.0 + jnp.exp(-jnp.abs(dtr)))
    da128 = dt128 * (-jnp.exp(alog_ref[...]))
    acs128 = jnp.dot(tril_ref[...], da128, preferred_element_type=F32, precision=HIGHEST)
    acs_t = acs128.T

    lane = lax.broadcasted_iota(jnp.int32, (L, LANES), 1)
    lo64 = lane < 64

    def col(x, h):
        return jnp.broadcast_to(x[:, h:h + 1], (L, LANES))

    def wide(x):
        return jnp.concatenate([jnp.where(lo64, col(x, 2 * m), col(x, 2 * m + 1)) for m in range(SSM_HEADS // 2)],
                               axis=1)

    dt_w = wide(dt128)
    acs_w = wide(acs128)
    aend_w = acs_w[L - 1:L, :]
    xdt = xs * dt_w
    xdt_dec = (xdt * jnp.exp(aend_w - acs_w)).astype(BF16)
    xdt_bf = xdt.astype(BF16)
    eacs_w = jnp.exp(acs_w)
    eaend_w = jnp.exp(aend_w)

    tri = lax.broadcasted_iota(jnp.int32, (L, L), 1) <= lax.broadcasted_iota(jnp.int32, (L, L), 0)

    def decay(h):
        diff = col(acs128, h) - jnp.broadcast_to(acs_t[h:h + 1, :], (L, L))
        return jnp.where(tri, jnp.exp(jnp.minimum(diff, 0.0)), 0.0)

    ys = []
    for g in range(SSM_GROUPS):
        bm = xc[:, SSM_WIDTH + g * SSM_STATE:SSM_WIDTH + (g + 1) * SSM_STATE]
        cm = xc[:, SSM_WIDTH + (SSM_GROUPS + g) * SSM_STATE:SSM_WIDTH + (SSM_GROUPS + g + 1) * SSM_STATE]
        cm_bf = cm.astype(BF16)
        cb = _dot_nt(cm_bf, bm.astype(BF16))
        st_g = st_ref[:, g * HW:(g + 1) * HW]
        y_off = _dot(cm_bf, st_g.astype(BF16)) * eacs_w[:, g * HW:(g + 1) * HW]
        new_st = _dot(bm.T.astype(BF16), xdt_dec[:, g * HW:(g + 1) * HW])
        st_ref[:, g * HW:(g + 1) * HW] = st_g * eaend_w[:, g * HW:(g + 1) * HW] + new_st
        slabs = []
        for mm in range(HW // LANES):
            m = g * (HW // LANES) + mm
            xslab = xdt_bf[:, m * LANES:(m + 1) * LANES]
            y0 = _dot((cb * decay(2 * m)).astype(BF16), xslab)
            y1 = _dot((cb * decay(2 * m + 1)).astype(BF16), xslab)
            slabs.append(jnp.where(lo64, y0, y1))
        ys.append(jnp.concatenate(slabs, axis=1) + y_off)
    y = jnp.concatenate(ys, axis=1) + xs * dsk_ref[...]
    z = z_ref[...]
    y = y * (z * _sigmoid(z))
    outs = []
    for g in range(SSM_GROUPS):
        yg = y[:, g * HW:(g + 1) * HW]
        yn = yg * lax.rsqrt(jnp.mean(yg * yg, axis=-1, keepdims=True) + EPS)
        outs.append(yn * ng_ref[:, g * HW:(g + 1) * HW])
    o_ref[...] = jnp.concatenate(outs, axis=1).astype(BF16)


def _ssd(proj, conv_w, conv_b, dt_bias, a_log, d_skip, norm_g):
    bsz, seq, _ = proj.shape
    pad = jnp.zeros((LANES - SSM_HEADS,), F32)
    dtb = jnp.concatenate([dt_bias, pad]).reshape(1, LANES)
    alog = jnp.concatenate([a_log, pad]).reshape(1, LANES)
    dsk = jnp.repeat(d_skip, SSM_HEADDIM).reshape(1, SSM_WIDTH)
    tril = jnp.tril(jnp.ones((SSM_CHUNK, SSM_CHUNK), F32))
    col = lambda width, off: pl.BlockSpec((None, SSM_CHUNK, width), lambda b, c: (b, c, off // width))
    full = lambda r, w: pl.BlockSpec((r, w), lambda b, c: (0, 0))
    return pl.pallas_call(
        _ssd_kernel,
        out_shape=jax.ShapeDtypeStruct((bsz, seq, SSM_WIDTH), BF16),
        grid=(bsz, seq // SSM_CHUNK),
        in_specs=[
            col(CONV_CH, P_XBC), col(SSM_WIDTH, P_Z), col(LANES, P_DT),
            full(SSM_CONV, CONV_CH), full(1, CONV_CH), full(1, LANES), full(1, LANES),
            full(1, SSM_WIDTH), full(1, SSM_WIDTH), full(SSM_CHUNK, SSM_CHUNK),
        ],
        out_specs=pl.BlockSpec((None, SSM_CHUNK, SSM_WIDTH), lambda b, c: (b, c, 0)),
        scratch_shapes=[
            pltpu.VMEM((SSM_CHUNK + 8, CONV_CH), F32),
            pltpu.VMEM((SSM_STATE, SSM_WIDTH), F32),
        ],
        compiler_params=_params(("arbitrary", "arbitrary")),
        name="ssd",
    )(proj, proj, proj, conv_w, conv_b.reshape(1, CONV_CH), dtb, alog, dsk, norm_g.reshape(1, SSM_WIDTH), tril)


def kernel(x, c, w_ada, b_ada, norm_g, w_ffn_gu, w_ffn_down, w_in, w_out, lb_logits, hg_norm_g, idx_k_norm_g,
           idx_k_norm_b, conv_w, conv_b, dt_bias, a_log, d_skip, ssm_norm_g, final_norm_g):
    bsz, seq, _ = x.shape
    mod4 = _ada(c, w_ada, b_ada).reshape(DEPTH, bsz, 9, D_MODEL)
    cos, sin = _rope_tables(seq)
    fg = final_norm_g.reshape(1, D_MODEL)
    for l in range(DEPTH):
        x = _ffn(x, mod4, norm_g, w_ffn_gu[l, 0].astype(BF16), w_ffn_down[l, 0].astype(BF16), fg, l, 0, False)
        proj = _inproj(x, mod4, norm_g, _pack_w_in(w_in[l]), l)
        o_a = _hgrn2(proj, lb_logits, hg_norm_g, l)
        o_b = _dsa(proj, cos, sin, idx_k_norm_g, idx_k_norm_b, l)
        o_c = _ssd(proj, conv_w[l], conv_b[l], dt_bias[l], a_log[l], d_skip[l], ssm_norm_g[l])
        x = _outproj(x, o_a, o_b, o_c, mod4, w_out[l].astype(BF16), l)
        x = _ffn(x, mod4, norm_g, w_ffn_gu[l, 1].astype(BF16), w_ffn_down[l, 1].astype(BF16), fg, l, 2,
                 l == DEPTH - 1)
    return x
```

```python
import functools

import numpy as np
import jax
import jax.numpy as jnp
from jax import lax
from jax.experimental import pallas as pl
from jax.experimental.pallas import tpu as pltpu

F32 = jnp.float32
BF16 = jnp.bfloat16
HIGHEST = lax.Precision.HIGHEST

D_MODEL = 1024
DEPTH = 2
HG_HEADS = 4
HG_DIM = 64
HG_WIDTH = HG_HEADS * HG_DIM
HG_CHUNK = 64
HG_SUB = 16
ATT_HEADS = 4
ATT_DIM = 64
ATT_WIDTH = ATT_HEADS * ATT_DIM
IDX_HEADS = 8
IDX_DIM = 64
TOPK_MAX = 256
Q_BLOCK = 128
ROPE_THETA = 10000.0
SSM_HEADS = 8
SSM_HEADDIM = 64
SSM_WIDTH = SSM_HEADS * SSM_HEADDIM
SSM_GROUPS = 2
SSM_STATE = 128
SSM_CONV = 4
SSM_CHUNK = 128
CONV_CH = SSM_WIDTH + 2 * SSM_GROUPS * SSM_STATE
MIX_WIDTH = HG_WIDTH + ATT_WIDTH + SSM_WIDTH
D_FF = 2816
EPS = 1e-6
NEG_BIG = -1e30
INT_MIN = -(2 ** 31)

LANES = 128
VMEM_LIMIT_BYTES = 56 * 1024 * 1024

P_HG = 0
P_IQ = 1024
P_AQ = 1536
P_KV = 1792
P_XBC = 2048
P_Z = 3072
P_DT = 3584
P_TOTAL = 3712

MIX_BATCH = 2
DSA_CLASSES = 8
DSA_ROWS = 128
COUNT_ROWS = 64
TOK_TILE = 512
FFN_CHUNK = 1408


def _sigmoid(x):
    return 1.0 / (1.0 + jnp.exp(-x))


def _norm_mod(x, g, shift, scale):
    y = x * lax.rsqrt(jnp.mean(x * x, axis=-1, keepdims=True) + EPS)
    return (y * g) * (1.0 + scale) + shift


def _dot(a, b):
    return jnp.dot(a, b, preferred_element_type=F32)


def _dot_nt(a, b):
    return lax.dot_general(a, b, (((1,), (1,)), ((), ())), preferred_element_type=F32)


def _dot_tn(a, b):
    return lax.dot_general(a, b, (((0,), (0,)), ((), ())), preferred_element_type=F32)


def _params(sem):
    return pltpu.CompilerParams(dimension_semantics=sem, vmem_limit_bytes=VMEM_LIMIT_BYTES)


def _const_spec(shape):
    nd = len(shape)
    return pl.BlockSpec(shape, lambda *_: (0,) * nd, pipeline_mode=pl.Buffered(1))


def _ada_kernel(c_ref, w_ref, b_ref, o_ref):
    c = c_ref[...]
    cond = c * _sigmoid(c)
    o_ref[...] = jnp.dot(cond, w_ref[...], preferred_element_type=F32, precision=HIGHEST) + b_ref[...]


def _ada(c, w_ada, b_ada):
    bsz = c.shape[0]
    n_mod = w_ada.shape[-1]
    nt = n_mod // D_MODEL
    return pl.pallas_call(
        _ada_kernel,
        out_shape=jax.ShapeDtypeStruct((DEPTH, bsz, n_mod), F32),
        grid=(DEPTH, nt),
        in_specs=[
            pl.BlockSpec((bsz, D_MODEL), lambda l, n: (0, 0)),
            pl.BlockSpec((None, D_MODEL, D_MODEL), lambda l, n: (l, 0, n)),
            pl.BlockSpec((None, 1, D_MODEL), lambda l, n: (l, 0, n)),
        ],
        out_specs=pl.BlockSpec((None, bsz, D_MODEL), lambda l, n: (l, 0, n)),
        compiler_params=_params(("arbitrary", "arbitrary")),
        name="ada_mod",
    )(c, w_ada, b_ada.reshape(DEPTH, 1, n_mod))


def _ffn_kernel(x_ref, mod_ref, ng_ref, wgu_ref, wd_ref, fg_ref, o_ref, act_ref, *, sub, final):
    x = x_ref[...]
    h = _norm_mod(x, ng_ref[sub:sub + 1, :], mod_ref[3 * sub:3 * sub + 1, :], mod_ref[3 * sub + 1:3 * sub + 2, :])
    h = h.astype(BF16)
    for c0 in range(0, D_FF, FFN_CHUNK):
        gate = _dot(h, wgu_ref[:, c0:c0 + FFN_CHUNK])
        up = _dot(h, wgu_ref[:, D_FF + c0:D_FF + c0 + FFN_CHUNK])
        act_ref[:, c0:c0 + FFN_CHUNK] = (gate * _sigmoid(gate) * up).astype(BF16)
    out = _dot(act_ref[...], wd_ref[...])
    y = x + (0.5 * mod_ref[3 * sub + 2:3 * sub + 3, :]) * out
    if final:
        y = (y * lax.rsqrt(jnp.mean(y * y, axis=-1, keepdims=True) + EPS)) * fg_ref[...]
    o_ref[...] = y


def _ffn(x, mod4, norm_g, wgu, wd, final_g, layer, sub, final):
    bsz, seq, _ = x.shape
    kern = functools.partial(_ffn_kernel, sub=sub, final=final)
    return pl.pallas_call(
        kern,
        out_shape=jax.ShapeDtypeStruct(x.shape, F32),
        grid=(bsz, seq // TOK_TILE),
        in_specs=[
            pl.BlockSpec((None, TOK_TILE, D_MODEL), lambda b, i: (b, i, 0)),
            pl.BlockSpec((None, None, 9, D_MODEL), lambda b, i: (layer, b, 0, 0)),
            pl.BlockSpec((None, 3, D_MODEL), lambda b, i: (layer, 0, 0)),
            _const_spec((D_MODEL, 2 * D_FF)),
            _const_spec((D_FF, D_MODEL)),
            _const_spec((1, D_MODEL)),
        ],
        out_specs=pl.BlockSpec((None, TOK_TILE, D_MODEL), lambda b, i: (b, i, 0)),
        scratch_shapes=[pltpu.VMEM((TOK_TILE, D_FF), BF16)],
        compiler_params=_params(("arbitrary", "arbitrary")),
        name="ffn",
    )(x, mod4, norm_g, wgu, wd, final_g)


def _inproj_kernel(x_ref, mod_ref, ng_ref, w_ref, o_ref):
    h = _norm_mod(x_ref[...], ng_ref[1:2, :], mod_ref[3:4, :], mod_ref[4:5, :]).astype(BF16)
    o_ref[...] = _dot(h, w_ref[...])


def _inproj(x, mod4, norm_g, w_packed, layer):
    bsz, seq, _ = x.shape
    return pl.pallas_call(
        _inproj_kernel,
        out_shape=jax.ShapeDtypeStruct((bsz, seq, P_TOTAL), F32),
        grid=(bsz, seq // TOK_TILE),
        in_specs=[
            pl.BlockSpec((None, TOK_TILE, D_MODEL), lambda b, i: (b, i, 0)),
            pl.BlockSpec((None, None, 9, D_MODEL), lambda b, i: (layer, b, 0, 0)),
            pl.BlockSpec((None, 3, D_MODEL), lambda b, i: (layer, 0, 0)),
            _const_spec((D_MODEL, P_TOTAL)),
        ],
        out_specs=pl.BlockSpec((None, TOK_TILE, P_TOTAL), lambda b, i: (b, i, 0)),
        compiler_params=_params(("arbitrary", "arbitrary")),
        name="in_proj",
    )(x, mod4, norm_g, w_packed)


def _pack_w_in(w):
    z = lambda n: jnp.zeros((w.shape[0], n), w.dtype)
    cols = [
        w[:, 0:1024],
        w[:, 1408:1920],
        w[:, 1024:1280],
        w[:, 1280:1344],
        w[:, 1920:1984],
        w[:, 1344:1408],
        w[:, 1984:1992],
        z(56),
        w[:, 2504:3528],
        w[:, 1992:2504],
        w[:, 3528:3536],
        z(120),
    ]
    return jnp.concatenate(cols, axis=1).astype(BF16)


def _outproj_kernel(x_ref, oa_ref, ob_ref, oc_ref, mod_ref, w_ref, o_ref):
    acc = _dot(oa_ref[...], w_ref[0:HG_WIDTH, :])
    acc += _dot(ob_ref[...], w_ref[HG_WIDTH:HG_WIDTH + ATT_WIDTH, :])
    acc += _dot(oc_ref[...], w_ref[HG_WIDTH + ATT_WIDTH:MIX_WIDTH, :])
    o_ref[...] = x_ref[...] + mod_ref[5:6, :] * acc


def _outproj(x, o_a, o_b, o_c, mod4, w_out, layer):
    bsz, seq, _ = x.shape
    tile = lambda w: pl.BlockSpec((None, TOK_TILE, w), lambda b, i: (b, i, 0))
    return pl.pallas_call(
        _outproj_kernel,
        out_shape=jax.ShapeDtypeStruct(x.shape, F32),
        grid=(bsz, seq // TOK_TILE),
        in_specs=[
            tile(D_MODEL), tile(HG_WIDTH), tile(ATT_WIDTH), tile(SSM_WIDTH),
            pl.BlockSpec((None, None, 9, D_MODEL), lambda b, i: (layer, b, 0, 0)),
            _const_spec((MIX_WIDTH, D_MODEL)),
        ],
        out_specs=tile(D_MODEL),
        compiler_params=_params(("arbitrary", "arbitrary")),
        name="out_proj",
    )(x, o_a, o_b, o_c, mod4, w_out)


def _split3(x):
    hi = x.astype(BF16)
    r = x - hi.astype(F32)
    mid = r.astype(BF16)
    lo = (r - mid.astype(F32)).astype(BF16)
    return hi, mid, lo


def _hgrn2_kernel(blk_ref, lbl_ref, ng_ref, tril_ref, bones_ref, o_ref, st_ref, p_ref, *, layer):
    @pl.when(pl.program_id(1) == 0)
    def _():
        st_ref[...] = jnp.zeros_like(st_ref)

    lg = lbl_ref[...]
    e = jnp.exp(lg - jnp.max(lg, axis=0, keepdims=True))
    sm = e / jnp.sum(e, axis=0, keepdims=True)
    cs = sm[0:1, :]
    for i in range(1, layer + 1):
        cs = cs + sm[i:i + 1, :]
    lb = cs - sm[0:1, :]

    for s in range(blk_ref.shape[0]):
        _hgrn2_chunk(blk_ref.at[s], lb, ng_ref, tril_ref, bones_ref, o_ref.at[s], st_ref.at[s], p_ref.at[s])


def _hgrn2_chunk(blk_ref, lb, ng_ref, tril_ref, bones_ref, o_ref, st_ref, p_ref):
    C, W, SB = HG_CHUNK, HG_WIDTH, HG_SUB
    q = blk_ref[:, 0:W]
    fl = blk_ref[:, W:2 * W]
    iv = blk_ref[:, 2 * W:3 * W]
    g = blk_ref[:, 3 * W:4 * W]
    f = lb + (1.0 - lb) * _sigmoid(fl)
    logf = jnp.log(jnp.maximum(f, 1e-30))
    kk = (1.0 - lb) * _sigmoid(-fl)
    b = jnp.dot(tril_ref[...], logf, preferred_element_type=F32, precision=HIGHEST)

    lane = lax.broadcasted_iota(jnp.int32, (1, W), 1)
    hmask = [(lane >= h * HG_DIM) & (lane < (h + 1) * HG_DIM) for h in range(HG_HEADS)]
    bones = bones_ref[...]
    iv_bf = iv.astype(BF16)

    rmod = lax.broadcasted_iota(jnp.int32, (C, W), 0) & (SB - 1)
    for d in range(SB):
        if d == 0:
            p = q * kk
        else:
            valid = rmod >= d
            dec = jnp.exp(jnp.where(valid, b - pltpu.roll(b, d, 0), 0.0))
            p = jnp.where(valid, q * pltpu.roll(kk, d, 0) * dec, 0.0)
        p_ref[d * C:(d + 1) * C, :] = p.astype(BF16)
    wsum = _dot(p_ref[...], bones)
    o = wsum[0:C, :] * iv
    for d in range(1, SB):
        o = o + wsum[d * C:(d + 1) * C, :] * pltpu.roll(iv, d, 0)

    parts = [jnp.zeros((SB, W), F32)]
    for i_sub in range(1, C // SB):
        r0 = i_sub * SB
        bref = b[r0 - 1:r0, :]
        qp = q[r0:r0 + SB, :] * jnp.exp(b[r0:r0 + SB, :] - bref)
        kp = (kk[0:r0, :] * jnp.exp(bref - b[0:r0, :])).astype(BF16)
        qstk = jnp.concatenate([jnp.where(hmask[h], qp, 0.0) for h in range(HG_HEADS)], axis=0).astype(BF16)
        sc = _dot_nt(qstk, kp)
        oi = _dot(sc.astype(BF16), iv_bf[0:r0, :])
        acc = jnp.where(hmask[0], oi[0:SB, :], 0.0)
        for h in range(1, HG_HEADS):
            acc = acc + jnp.where(hmask[h], oi[h * SB:(h + 1) * SB, :], 0.0)
        parts.append(acc)
    o = o + jnp.concatenate(parts, axis=0)

    st = st_ref[...]
    o = o + _dot_nt((q * jnp.exp(b)).astype(BF16), st.astype(BF16))
    b_end = b[C - 1:C, :]
    kd = (kk * jnp.exp(b_end - b)).astype(BF16)
    upd = _dot_tn(iv_bf, kd)
    st_ref[...] = st * jnp.exp(b_end) + jnp.where(bones > 0, upd, 0.0)

    hi, mid, lo = _split3(o * o)
    ss = _dot(jnp.concatenate([hi, mid, lo], axis=0), bones)
    ms = (ss[0:C, :] + ss[C:2 * C, :] + ss[2 * C:3 * C, :]) * (1.0 / HG_DIM)
    o = (o * lax.rsqrt(ms + EPS)) * ng_ref[...]
    o_ref[...] = (o * (g * _sigmoid(g))).astype(BF16)


def _hgrn2(proj, lb_logits, norm_g, layer):
    bsz, seq, _ = proj.shape
    tril = jnp.tril(jnp.ones((HG_CHUNK, HG_CHUNK), F32))
    head = np.arange(HG_WIDTH) // HG_DIM
    bones = jnp.asarray(head[:, None] == head[None, :], BF16)
    kern = functools.partial(_hgrn2_kernel, layer=layer)
    return pl.pallas_call(
        kern,
        out_shape=jax.ShapeDtypeStruct((bsz, seq, HG_WIDTH), BF16),
        grid=(bsz // MIX_BATCH, seq // HG_CHUNK),
        in_specs=[
            pl.BlockSpec((MIX_BATCH, HG_CHUNK, 4 * HG_WIDTH), lambda b, c: (b, c, P_HG // (4 * HG_WIDTH))),
            pl.BlockSpec((DEPTH, HG_WIDTH), lambda b, c: (0, 0)),
            pl.BlockSpec((1, HG_WIDTH), lambda b, c: (0, 0)),
            pl.BlockSpec((HG_CHUNK, HG_CHUNK), lambda b, c: (0, 0)),
            pl.BlockSpec((HG_WIDTH, HG_WIDTH), lambda b, c: (0, 0)),
        ],
        out_specs=pl.BlockSpec((MIX_BATCH, HG_CHUNK, HG_WIDTH), lambda b, c: (b, c, 0)),
        scratch_shapes=[
            pltpu.VMEM((MIX_BATCH, HG_WIDTH, HG_WIDTH), F32),
            pltpu.VMEM((MIX_BATCH, HG_SUB * HG_CHUNK, HG_WIDTH), BF16),
        ],
        compiler_params=_params(("arbitrary", "arbitrary")),
        name="hgrn2",
    )(proj, lb_logits, norm_g[layer].reshape(1, HG_WIDTH), tril, bones)


_BIT_SWAP_MASK = {16: 0x0000FFFF, 8: 0x00FF00FF, 4: 0x0F0F0F0F, 2: 0x33333333, 1: 0x55555555}


def _dsa_kernel(iq_ref, aq_ref, kv_ref, cos_ref, sin_ref, lng_ref, lnb_ref, o_ref,
                kidx_ref, kr_ref, vt_ref, l2_ref, score_ref, plane_ref, p_ref, thr_ref, cnt_ref, sg_ref,
                *, n_keep, n_blocks):
    j = pl.program_id(1)
    QB = Q_BLOCK
    lane = lax.broadcasted_iota(jnp.int32, (QB, LANES), 1)
    lo64 = lane < 64
    first_half = (lane & 63) < 32
    cos = cos_ref[...]
    sin = sin_ref[...]

    def rope(x):
        rot = jnp.where(first_half, pltpu.roll(x, 96, 1), pltpu.roll(x, 32, 1))
        return x * cos + rot * sin

    def head_pair(x):
        return jnp.concatenate([jnp.where(lo64, x, 0.0), jnp.where(lo64, 0.0, x)], axis=0).astype(BF16)

    @pl.when(j == 0)
    def _():
        kidx_ref[...] = jnp.zeros_like(kidx_ref)
        kr_ref[...] = jnp.zeros_like(kr_ref)
        vt_ref[...] = jnp.zeros_like(vt_ref)

    a = kv_ref[:, 0:LANES]
    mu = jnp.sum(jnp.where(lo64, 0.0, a), axis=-1, keepdims=True) * (1.0 / IDX_DIM)
    xc = a - mu
    var = jnp.sum(jnp.where(lo64, 0.0, xc * xc), axis=-1, keepdims=True) * (1.0 / IDX_DIM)
    y = (xc * lax.rsqrt(var + EPS)) * lng_ref[...] + lnb_ref[...]
    kr = rope(jnp.where(lo64, a, y))
    krs = pltpu.roll(kr, 64, 1)
    kr_ref[j] = jnp.where(lo64, kr, krs).astype(BF16)
    kidx_ref[j] = jnp.where(lo64, krs, kr).astype(BF16)
    bvt = kv_ref[:, LANES:2 * LANES].T
    vt_ref[j] = bvt[0:ATT_DIM, :].astype(BF16)
    w_t = bvt[ATT_DIM:ATT_DIM + IDX_HEADS, :] * (IDX_HEADS ** -0.5 * IDX_DIM ** -0.5)

    def block(nb):
        nk = nb * QB
        nt = nk // DSA_ROWS
        row_iota = lax.broadcasted_iota(jnp.int32, (DSA_ROWS, QB), 0)
        tpos = j * QB + lax.broadcasted_iota(jnp.int32, (DSA_ROWS, QB), 1)

        def rows(i):
            return pl.ds(pl.multiple_of(i * DSA_ROWS, DSA_ROWS), DSA_ROWS)

        def fold(x, op):
            y = x.reshape(DSA_ROWS // 8, 8, QB)
            acc = y[0]
            for r in range(1, DSA_ROWS // 8):
                acc = op(acc, y[r])
            return acc

        kidx = kidx_ref[0:nb].reshape(nk, LANES)
        for m in range(IDX_HEADS // 2):
            qr = rope(iq_ref[:, m * LANES:(m + 1) * LANES])
            l2_ref[m, 0:nk, :] = _dot_nt(kidx, head_pair(qr))

        def score_tile(i, carry):
            t = None
            for m in range(IDX_HEADS // 2):
                u = (jnp.maximum(l2_ref[m, rows(i), 0:QB], 0.0) * w_t[2 * m:2 * m + 1, :]
                     + jnp.maximum(l2_ref[m, rows(i), QB:2 * QB], 0.0) * w_t[2 * m + 1:2 * m + 2, :])
                t = u if t is None else t + u
            t = jnp.where(t == 0.0, 0.0, t)
            score_ref[rows(i), :] = jnp.where(i * DSA_ROWS + row_iota <= tpos, t, -jnp.inf)
            return carry

        lax.fori_loop(0, nt, score_tile, 0)

        def count(pred_fn):
            acc = jnp.zeros((COUNT_ROWS, QB), F32)
            for r0 in range(0, nk, COUNT_ROWS):
                acc = acc + jnp.where(pred_fn(score_ref[r0:r0 + COUNT_ROWS, :], r0), 1.0, 0.0)
            return jnp.sum(acc, axis=0, keepdims=True)

        def ordered_bits(v):
            return v ^ ((v >> 31) & jnp.int32(0x7FFFFFFF))

        def key_value(k):
            return lax.bitcast_convert_type(ordered_bits(k), F32)

        @pl.when(j * QB < n_keep)
        def _():
            thr_ref[...] = jnp.full((1, QB), -jnp.inf, F32)
            sg_ref[...] = jnp.full((1, QB), -1, jnp.int32)

        @pl.when(j * QB >= n_keep)
        def _():
            n_groups = nk // (32 * 8)

            def plane_group(g, carry):
                base = pl.multiple_of(g * 256, 256)
                x = [ordered_bits(lax.bitcast_convert_type(score_ref[pl.ds(base + 8 * i, 8), :], jnp.int32))
                     ^ jnp.int32(INT_MIN) for i in range(32)]
                step = 16
                while step:
                    msk = jnp.int32(_BIT_SWAP_MASK[step])
                    for k in range(32):
                        if k & step == 0:
                            a, b = x[k], x[k | step]
                            t = (lax.shift_right_logical(a, jnp.int32(step)) ^ b) & msk
                            x[k | step] = b ^ t
                            x[k] = a ^ lax.shift_left(t, jnp.int32(step))
                    step >>= 1
                for b in range(32):
                    plane_ref[b, pl.ds(pl.multiple_of(g * 8, 8), 8), :] = x[b]
                return carry

            lax.fori_loop(0, n_groups, plane_group, 0)

            def popsum(words):
                pc = lax.population_count(words).reshape(n_groups, 8, QB)
                acc = pc[0]
                for g in range(1, n_groups):
                    acc = acc + pc[g]
                return jnp.sum(acc.astype(F32), axis=0, keepdims=True).astype(jnp.int32)

            def body(i, carry):
                active, need, thr = carry
                hit = active & plane_ref[31 - i, 0:8 * n_groups, :]
                ones = popsum(hit)
                take = ones >= need
                active = jnp.where(take, hit, active ^ hit)
                need = jnp.where(take, need, need - ones)
                thr = jnp.where(take, thr | lax.shift_left(jnp.int32(1), 31 - i), thr)
                return active, need, thr

            _, _, thr_key = lax.fori_loop(
                0, 32, body,
                (jnp.full((8 * n_groups, QB), -1, jnp.int32), jnp.full((1, QB), n_keep, jnp.int32),
                 jnp.zeros((1, QB), jnp.int32)))
            thr = key_value(thr_key ^ jnp.int32(INT_MIN))

            def rank_counts(t):
                return count(lambda s, r0: s > t), count(lambda s, r0: s >= t)

            n_gt, n_ge = rank_counts(thr)
            thr_ref[...] = thr
            cnt_ref[0:1, :] = n_gt
            cnt_ref[1:2, :] = n_ge
            wrong = jnp.max(jnp.where(n_gt < n_keep, jnp.where(n_ge >= n_keep, 0.0, 1.0), 1.0))

            @pl.when(wrong > 0.5)
            def _():
                n_pos = count(lambda s, r0: s >= 0.0)
                lo0 = jnp.where(n_pos >= n_keep, jnp.int32(0), jnp.int32(INT_MIN))

                def bisect(i, lo):
                    cand = lo | lax.shift_left(jnp.int32(1), 30 - i)
                    t = key_value(cand)
                    return jnp.where(count(lambda s, r0: s >= t) >= n_keep, cand, lo)

                t = key_value(lax.fori_loop(0, 31, bisect, lo0))
                n_gt2, n_ge2 = rank_counts(t)
                thr_ref[...] = t
                cnt_ref[0:1, :] = n_gt2
                cnt_ref[1:2, :] = n_ge2

            thr = thr_ref[...]
            need = n_keep - cnt_ref[0:1, :]
            n_eq = cnt_ref[1:2, :] - cnt_ref[0:1, :]
            sg_ref[...] = jnp.full((1, QB), nk, jnp.int32)

            @pl.when(jnp.max(n_eq - need) > 0.5)
            def _():
                crow = lax.broadcasted_iota(jnp.int32, (COUNT_ROWS, QB), 0)

                def body2(i, sg):
                    cand = sg | lax.shift_left(jnp.int32(1), 10 - i)
                    cnt = count(lambda s, r0: jnp.where(r0 + crow < cand, s, -jnp.inf) == thr)
                    return jnp.where(cnt < need, cand, sg)

                sg_ref[...] = lax.fori_loop(0, 11, body2, jnp.zeros((1, QB), jnp.int32))

        krd = kr_ref[0:nb].reshape(nk, LANES)
        for m in range(ATT_HEADS // 2):
            qr = rope(aq_ref[:, m * LANES:(m + 1) * LANES]) * (ATT_DIM ** -0.5)
            l2_ref[m, 0:nk, :] = _dot_nt(krd, head_pair(qr))
        thr = thr_ref[...]
        sg = sg_ref[...]

        def head_cols(h):
            return h // 2, slice((h % 2) * QB, (h % 2 + 1) * QB)

        def mask_tile(i, mx):
            sc = score_ref[rows(i), :]
            tie = jnp.where(i * DSA_ROWS + row_iota <= sg, jnp.inf, NEG_BIG)
            bias = jnp.where(sc > thr, jnp.inf, jnp.where(sc == thr, tie, NEG_BIG))
            out = []
            for h in range(ATT_HEADS):
                m, cols = head_cols(h)
                s = jnp.minimum(l2_ref[m, rows(i), cols], bias)
                l2_ref[m, rows(i), cols] = s
                out.append(jnp.maximum(mx[h], fold(s, jnp.maximum)))
            return tuple(out)

        mx = lax.fori_loop(0, nt, mask_tile, tuple(jnp.full((8, QB), -jnp.inf, F32) for _ in range(ATT_HEADS)))
        mx = [jnp.max(v, axis=0, keepdims=True) for v in mx]

        def prob_tile(i, ls):
            out = []
            for h in range(ATT_HEADS):
                m, cols = head_cols(h)
                p = jnp.exp(l2_ref[m, rows(i), cols] - mx[h])
                p_ref[rows(i), h * QB:(h + 1) * QB] = p.astype(BF16)
                out.append(ls[h] + fold(p, jnp.add))
            return tuple(out)

        ls = lax.fori_loop(0, nt, prob_tile, tuple(jnp.zeros((8, QB), F32) for _ in range(ATT_HEADS)))
        vt = jnp.concatenate([vt_ref[i] for i in range(nb)], axis=1)
        o4 = _dot(vt, p_ref[0:nk, :])
        outs = [o4[:, h * QB:(h + 1) * QB] * (1.0 / jnp.sum(ls[h], axis=0, keepdims=True))
                for h in range(ATT_HEADS)]
        o_ref[...] = jnp.concatenate(outs, axis=0).T.astype(BF16)

    per_class = n_blocks // DSA_CLASSES
    for cls in range(DSA_CLASSES):
        @pl.when((j >= cls * per_class) & (j < (cls + 1) * per_class))
        def _(cls=cls):
            block((cls + 1) * per_class)


def _rope_tables(seq):
    inv_freq = 1.0 / (ROPE_THETA ** (jnp.arange(0, ATT_DIM, 2, dtype=F32) / ATT_DIM))
    ang = jnp.arange(seq, dtype=F32)[:, None] * inv_freq[None, :]
    cos, sin = jnp.cos(ang), jnp.sin(ang)
    return jnp.concatenate([cos] * 4, axis=1), jnp.concatenate([-sin, sin, -sin, sin], axis=1)


def _dsa(proj, cos, sin, kn_g, kn_b, layer):
    bsz, seq, _ = proj.shape
    n_blocks = seq // Q_BLOCK
    n_keep = min(TOPK_MAX, seq // 4)
    pad = jnp.zeros((IDX_DIM,), F32)
    lng = jnp.concatenate([pad, kn_g[layer]]).reshape(1, LANES)
    lnb = jnp.concatenate([pad, kn_b[layer]]).reshape(1, LANES)
    kern = functools.partial(_dsa_kernel, n_keep=n_keep, n_blocks=n_blocks)
    col = lambda width, off: pl.BlockSpec((None, Q_BLOCK, width), lambda b, j: (b, j, off // width))
    return pl.pallas_call(
        kern,
        out_shape=jax.ShapeDtypeStruct((bsz, seq, ATT_WIDTH), BF16),
        grid=(bsz, n_blocks),
        in_specs=[
            col(IDX_HEADS * IDX_DIM, P_IQ), col(ATT_WIDTH, P_AQ), col(2 * LANES, P_KV),
            pl.BlockSpec((Q_BLOCK, LANES), lambda b, j: (j, 0)),
            pl.BlockSpec((Q_BLOCK, LANES), lambda b, j: (j, 0)),
            pl.BlockSpec((1, LANES), lambda b, j: (0, 0)),
            pl.BlockSpec((1, LANES), lambda b, j: (0, 0)),
        ],
        out_specs=pl.BlockSpec((None, Q_BLOCK, ATT_WIDTH), lambda b, j: (b, j, 0)),
        scratch_shapes=[
            pltpu.VMEM((n_blocks, Q_BLOCK, LANES), BF16),
            pltpu.VMEM((n_blocks, Q_BLOCK, LANES), BF16),
            pltpu.VMEM((n_blocks, ATT_DIM, Q_BLOCK), BF16),
            pltpu.VMEM((IDX_HEADS // 2, seq, 2 * Q_BLOCK), F32),
            pltpu.VMEM((seq, Q_BLOCK), F32),
            pltpu.VMEM((32, seq // 32, Q_BLOCK), jnp.int32),
            pltpu.VMEM((seq, ATT_HEADS * Q_BLOCK), BF16),
            pltpu.VMEM((1, Q_BLOCK), F32),
            pltpu.VMEM((2, Q_BLOCK), F32),
            pltpu.VMEM((1, Q_BLOCK), jnp.int32),
        ],
        compiler_params=_params(("arbitrary", "arbitrary")),
        name="dsa",
    )(proj, proj, proj, cos, sin, lng, lnb)


def _ssd_kernel(xbc_ref, z_ref, dt_ref, cw_ref, cb_ref, dtb_ref, alog_ref, dsk_ref, ng_ref, tril_ref,
                o_ref, xpad_ref, st_ref):
    @pl.when(pl.program_id(1) == 0)
    def _():
        xpad_ref[...] = jnp.zeros_like(xpad_ref)
        st_ref[...] = jnp.zeros_like(st_ref)

    for s in range(xbc_ref.shape[0]):
        _ssd_chunk(xbc_ref.at[s], z_ref.at[s], dt_ref.at[s], cw_ref, cb_ref, dtb_ref, alog_ref, dsk_ref, ng_ref,
                   tril_ref, o_ref.at[s], xpad_ref.at[s], st_ref.at[s])


def _ssd_chunk(xbc_ref, z_ref, dt_ref, cw_ref, cb_ref, dtb_ref, alog_ref, dsk_ref, ng_ref, tril_ref,
               o_ref, xpad_ref, st_ref):
    L = SSM_CHUNK
    HW = SSM_WIDTH // SSM_GROUPS

    x = xbc_ref[...]
    conv = cb_ref[...] + cw_ref[SSM_CONV - 1:SSM_CONV, :] * x
    for d in range(1, SSM_CONV):
        conv = conv + cw_ref[SSM_CONV - 1 - d:SSM_CONV - d, :] * pltpu.roll(x, d, 0)
    hx = jnp.concatenate([xpad_ref[...], x[0:8, :]], axis=0)
    head = cb_ref[...]
    for i in range(SSM_CONV):
        off = 8 - (SSM_CONV - 1) + i
        head = head + cw_ref[i:i + 1, :] * hx[off:off + 8, :]
    conv = jnp.concatenate([head, conv[8:, :]], axis=0)
    xpad_ref[...] = x[L - 8:L, :]
    xc = conv * _sigmoid(conv)
    xs = xc[:, 0:SSM_WIDTH]

    dtr = dt_ref[...] + dtb_ref[...]
    dt128 = jnp.maximum(dtr, 0.0) + jnp.log(1.0 + jnp.exp(-jnp.abs(dtr)))
    da128 = dt128 * (-jnp.exp(alog_ref[...]))
    acs128 = jnp.dot(tril_ref[...], da128, preferred_element_type=F32, precision=HIGHEST)
    acs_t = acs128.T

    lane = lax.broadcasted_iota(jnp.int32, (L, LANES), 1)
    lo64 = lane < 64

    def col(x, h):
        return jnp.broadcast_to(x[:, h:h + 1], (L, LANES))

    def wide(x):
        return jnp.concatenate([jnp.where(lo64, col(x, 2 * m), col(x, 2 * m + 1)) for m in range(SSM_HEADS // 2)],
                               axis=1)

    dt_w = wide(dt128)
    acs_w = wide(acs128)
    aend_w = acs_w[L - 1:L, :]
    xdt = xs * dt_w
    xdt_dec = (xdt * jnp.exp(aend_w - acs_w)).astype(BF16)
    xdt_bf = xdt.astype(BF16)
    eacs_w = jnp.exp(acs_w)
    eaend_w = jnp.exp(aend_w)

    tri = lax.broadcasted_iota(jnp.int32, (L, L), 1) <= lax.broadcasted_iota(jnp.int32, (L, L), 0)

    def decay(h):
        diff = col(acs128, h) - jnp.broadcast_to(acs_t[h:h + 1, :], (L, L))
        return jnp.where(tri, jnp.exp(jnp.minimum(diff, 0.0)), 0.0)

    ys = []
    for g in range(SSM_GROUPS):
        bm = xc[:, SSM_WIDTH + g * SSM_STATE:SSM_WIDTH + (g + 1) * SSM_STATE]
        cm = xc[:, SSM_WIDTH + (SSM_GROUPS + g) * SSM_STATE:SSM_WIDTH + (SSM_GROUPS + g + 1) * SSM_STATE]
        cm_bf = cm.astype(BF16)
        cb = _dot_nt(cm_bf, bm.astype(BF16))
        st_g = st_ref[:, g * HW:(g + 1) * HW]
        y_off = _dot(cm_bf, st_g.astype(BF16)) * eacs_w[:, g * HW:(g + 1) * HW]
        new_st = _dot(bm.T.astype(BF16), xdt_dec[:, g * HW:(g + 1) * HW])
        st_ref[:, g * HW:(g + 1) * HW] = st_g * eaend_w[:, g * HW:(g + 1) * HW] + new_st
        slabs = []
        for mm in range(HW // LANES):
            m = g * (HW // LANES) + mm
            xslab = xdt_bf[:, m * LANES:(m + 1) * LANES]
            y0 = _dot((cb * decay(2 * m)).astype(BF16), xslab)
            y1 = _dot((cb * decay(2 * m + 1)).astype(BF16), xslab)
            slabs.append(jnp.where(lo64, y0, y1))
        ys.append(jnp.concatenate(slabs, axis=1) + y_off)
    y = jnp.concatenate(ys, axis=1) + xs * dsk_ref[...]
    z = z_ref[...]
    y = y * (z * _sigmoid(z))
    outs = []
    for g in range(SSM_GROUPS):
        yg = y[:, g * HW:(g + 1) * HW]
        yn = yg * lax.rsqrt(jnp.mean(yg * yg, axis=-1, keepdims=True) + EPS)
        outs.append(yn * ng_ref[:, g * HW:(g + 1) * HW])
    o_ref[...] = jnp.concatenate(outs, axis=1).astype(BF16)


def _ssd(proj, conv_w, conv_b, dt_bias, a_log, d_skip, norm_g):
    bsz, seq, _ = proj.shape
    pad = jnp.zeros((LANES - SSM_HEADS,), F32)
    dtb = jnp.concatenate([dt_bias, pad]).reshape(1, LANES)
    alog = jnp.concatenate([a_log, pad]).reshape(1, LANES)
    dsk = jnp.repeat(d_skip, SSM_HEADDIM).reshape(1, SSM_WIDTH)
    tril = jnp.tril(jnp.ones((SSM_CHUNK, SSM_CHUNK), F32))
    col = lambda width, off: pl.BlockSpec((MIX_BATCH, SSM_CHUNK, width), lambda b, c: (b, c, off // width))
    full = lambda r, w: pl.BlockSpec((r, w), lambda b, c: (0, 0))
    return pl.pallas_call(
        _ssd_kernel,
        out_shape=jax.ShapeDtypeStruct((bsz, seq, SSM_WIDTH), BF16),
        grid=(bsz // MIX_BATCH, seq // SSM_CHUNK),
        in_specs=[
            col(CONV_CH, P_XBC), col(SSM_WIDTH, P_Z), col(LANES, P_DT),
            full(SSM_CONV, CONV_CH), full(1, CONV_CH), full(1, LANES), full(1, LANES),
            full(1, SSM_WIDTH), full(1, SSM_WIDTH), full(SSM_CHUNK, SSM_CHUNK),
        ],
        out_specs=pl.BlockSpec((MIX_BATCH, SSM_CHUNK, SSM_WIDTH), lambda b, c: (b, c, 0)),
        scratch_shapes=[
            pltpu.VMEM((MIX_BATCH, 8, CONV_CH), F32),
            pltpu.VMEM((MIX_BATCH, SSM_STATE, SSM_WIDTH), F32),
        ],
        compiler_params=_params(("arbitrary", "arbitrary")),
        name="ssd",
    )(proj, proj, proj, conv_w, conv_b.reshape(1, CONV_CH), dtb, alog, dsk, norm_g.reshape(1, SSM_WIDTH), tril)


def kernel(x, c, w_ada, b_ada, norm_g, w_ffn_gu, w_ffn_down, w_in, w_out, lb_logits, hg_norm_g, idx_k_norm_g,
           idx_k_norm_b, conv_w, conv_b, dt_bias, a_log, d_skip, ssm_norm_g, final_norm_g):
    bsz, seq, _ = x.shape
    mod4 = _ada(c, w_ada, b_ada).reshape(DEPTH, bsz, 9, D_MODEL)
    cos, sin = _rope_tables(seq)
    fg = final_norm_g.reshape(1, D_MODEL)
    for l in range(DEPTH):
        x = _ffn(x, mod4, norm_g, w_ffn_gu[l, 0].astype(BF16), w_ffn_down[l, 0].astype(BF16), fg, l, 0, False)
        proj = _inproj(x, mod4, norm_g, _pack_w_in(w_in[l]), l)
        o_a = _hgrn2(proj, lb_logits, hg_norm_g, l)
        o_b = _dsa(proj, cos, sin, idx_k_norm_g, idx_k_norm_b, l)
        o_c = _ssd(proj, conv_w[l], conv_b[l], dt_bias[l], a_log[l], d_skip[l], ssm_norm_g[l])
        x = _outproj(x, o_a, o_b, o_c, mod4, w_out[l].astype(BF16), l)
        x = _ffn(x, mod4, norm_g, w_ffn_gu[l, 1].astype(BF16), w_ffn_down[l, 1].astype(BF16), fg, l, 2,
                 l == DEPTH - 1)
    return x
```

```python
import functools

import numpy as np
import jax
import jax.numpy as jnp
from jax import lax
from jax.experimental import pallas as pl
from jax.experimental.pallas import tpu as pltpu

F32 = jnp.float32
BF16 = jnp.bfloat16
HIGHEST = lax.Precision.HIGHEST

D_MODEL = 1024
DEPTH = 2
HG_HEADS = 4
HG_DIM = 64
HG_WIDTH = HG_HEADS * HG_DIM
HG_CHUNK = 64
HG_SUB = 16
ATT_HEADS = 4
ATT_DIM = 64
ATT_WIDTH = ATT_HEADS * ATT_DIM
IDX_HEADS = 8
IDX_DIM = 64
TOPK_MAX = 256
Q_BLOCK = 128
ROPE_THETA = 10000.0
SSM_HEADS = 8
SSM_HEADDIM = 64
SSM_WIDTH = SSM_HEADS * SSM_HEADDIM
SSM_GROUPS = 2
SSM_STATE = 128
SSM_CONV = 4
SSM_CHUNK = 128
CONV_CH = SSM_WIDTH + 2 * SSM_GROUPS * SSM_STATE
MIX_WIDTH = HG_WIDTH + ATT_WIDTH + SSM_WIDTH
D_FF = 2816
EPS = 1e-6
NEG_BIG = -1e30
INT_MIN = -(2 ** 31)

LANES = 128
VMEM_LIMIT_BYTES = 56 * 1024 * 1024

P_HG = 0
P_IQ = 1024
P_AQ = 1536
P_KV = 1792
P_XBC = 2048
P_Z = 3072
P_DT = 3584
P_TOTAL = 3712

MIX_BATCH = 2
DSA_CLASSES = 8
DSA_ROWS = 128
COUNT_ROWS = 64
TOK_TILE = 512
FFN_TILE = 1024
FFN_CHUNK = 256


def _sigmoid(x):
    return 1.0 / (1.0 + jnp.exp(-x))


def _norm_mod(x, g, shift, scale):
    y = x * lax.rsqrt(jnp.mean(x * x, axis=-1, keepdims=True) + EPS)
    return (y * g) * (1.0 + scale) + shift


def _dot(a, b):
    return jnp.dot(a, b, preferred_element_type=F32)


def _dot_nt(a, b):
    return lax.dot_general(a, b, (((1,), (1,)), ((), ())), preferred_element_type=F32)


def _dot_tn(a, b):
    return lax.dot_general(a, b, (((0,), (0,)), ((), ())), preferred_element_type=F32)


def _params(sem):
    return pltpu.CompilerParams(dimension_semantics=sem, vmem_limit_bytes=VMEM_LIMIT_BYTES)


def _const_spec(shape):
    nd = len(shape)
    return pl.BlockSpec(shape, lambda *_: (0,) * nd, pipeline_mode=pl.Buffered(1))


def _ada_kernel(c_ref, w_ref, b_ref, o_ref):
    c = c_ref[...]
    cond = c * _sigmoid(c)
    o_ref[...] = jnp.dot(cond, w_ref[...], preferred_element_type=F32, precision=HIGHEST) + b_ref[...]


def _ada(c, w_ada, b_ada):
    bsz = c.shape[0]
    n_mod = w_ada.shape[-1]
    nt = n_mod // D_MODEL
    return pl.pallas_call(
        _ada_kernel,
        out_shape=jax.ShapeDtypeStruct((DEPTH, bsz, n_mod), F32),
        grid=(DEPTH, nt),
        in_specs=[
            pl.BlockSpec((bsz, D_MODEL), lambda l, n: (0, 0)),
            pl.BlockSpec((None, D_MODEL, D_MODEL), lambda l, n: (l, 0, n)),
            pl.BlockSpec((None, 1, D_MODEL), lambda l, n: (l, 0, n)),
        ],
        out_specs=pl.BlockSpec((None, bsz, D_MODEL), lambda l, n: (l, 0, n)),
        compiler_params=_params(("arbitrary", "arbitrary")),
        name="ada_mod",
    )(c, w_ada, b_ada.reshape(DEPTH, 1, n_mod))


def _ffn_kernel(x_ref, mod_ref, ng_ref, wgu_ref, wd_ref, fg_ref, o_ref, act_ref, *, sub, final):
    _ffn_body(x_ref[...], mod_ref, ng_ref, wgu_ref, wd_ref, fg_ref, o_ref, act_ref, sub, final)


def _mix_ffn_kernel(x_ref, oa_ref, ob_ref, oc_ref, wo_ref, mod_ref, ng_ref, wgu_ref, wd_ref, fg_ref, o_ref, act_ref,
                    *, final):
    acc = _dot(oa_ref[...], wo_ref[0:HG_WIDTH, :])
    acc += _dot(ob_ref[...], wo_ref[HG_WIDTH:HG_WIDTH + ATT_WIDTH, :])
    acc += _dot(oc_ref[...], wo_ref[HG_WIDTH + ATT_WIDTH:MIX_WIDTH, :])
    x = x_ref[...] + mod_ref[5:6, :] * acc
    _ffn_body(x, mod_ref, ng_ref, wgu_ref, wd_ref, fg_ref, o_ref, act_ref, 2, final)


def _ffn_body(x, mod_ref, ng_ref, wgu_ref, wd_ref, fg_ref, o_ref, act_ref, sub, final):
    h = _norm_mod(x, ng_ref[sub:sub + 1, :], mod_ref[3 * sub:3 * sub + 1, :], mod_ref[3 * sub + 1:3 * sub + 2, :])
    h = h.astype(BF16)
    for c0 in range(0, D_FF, FFN_CHUNK):
        gate = _dot(h, wgu_ref[:, c0:c0 + FFN_CHUNK])
        up = _dot(h, wgu_ref[:, D_FF + c0:D_FF + c0 + FFN_CHUNK])
        act_ref[:, c0:c0 + FFN_CHUNK] = (gate * _sigmoid(gate) * up).astype(BF16)
    out = _dot(act_ref[...], wd_ref[...])
    y = x + (0.5 * mod_ref[3 * sub + 2:3 * sub + 3, :]) * out
    if final:
        y = (y * lax.rsqrt(jnp.mean(y * y, axis=-1, keepdims=True) + EPS)) * fg_ref[...]
    o_ref[...] = y


def _ffn(x, mod4, norm_g, wgu, wd, final_g, layer, sub, final):
    bsz, seq, _ = x.shape
    kern = functools.partial(_ffn_kernel, sub=sub, final=final)
    return pl.pallas_call(
        kern,
        out_shape=jax.ShapeDtypeStruct(x.shape, F32),
        grid=(bsz, seq // FFN_TILE),
        in_specs=[
            pl.BlockSpec((None, FFN_TILE, D_MODEL), lambda b, i: (b, i, 0)),
            pl.BlockSpec((None, None, 9, D_MODEL), lambda b, i: (layer, b, 0, 0)),
            pl.BlockSpec((None, 3, D_MODEL), lambda b, i: (layer, 0, 0)),
            _const_spec((D_MODEL, 2 * D_FF)),
            _const_spec((D_FF, D_MODEL)),
            _const_spec((1, D_MODEL)),
        ],
        out_specs=pl.BlockSpec((None, FFN_TILE, D_MODEL), lambda b, i: (b, i, 0)),
        scratch_shapes=[pltpu.VMEM((FFN_TILE, D_FF), BF16)],
        compiler_params=_params(("arbitrary", "arbitrary")),
        name="ffn",
    )(x, mod4, norm_g, wgu, wd, final_g)


def _mix_ffn(x, o_a, o_b, o_c, w_out, mod4, norm_g, wgu, wd, final_g, layer, final):
    bsz, seq, _ = x.shape
    tile = lambda w: pl.BlockSpec((None, FFN_TILE, w), lambda b, i: (b, i, 0))
    return pl.pallas_call(
        functools.partial(_mix_ffn_kernel, final=final),
        out_shape=jax.ShapeDtypeStruct(x.shape, F32),
        grid=(bsz, seq // FFN_TILE),
        in_specs=[
            tile(D_MODEL), tile(HG_WIDTH), tile(ATT_WIDTH), tile(SSM_WIDTH),
            _const_spec((MIX_WIDTH, D_MODEL)),
            pl.BlockSpec((None, None, 9, D_MODEL), lambda b, i: (layer, b, 0, 0)),
            pl.BlockSpec((None, 3, D_MODEL), lambda b, i: (layer, 0, 0)),
            _const_spec((D_MODEL, 2 * D_FF)),
            _const_spec((D_FF, D_MODEL)),
            _const_spec((1, D_MODEL)),
        ],
        out_specs=tile(D_MODEL),
        scratch_shapes=[pltpu.VMEM((FFN_TILE, D_FF), BF16)],
        compiler_params=_params(("arbitrary", "arbitrary")),
        name="mix_ffn",
    )(x, o_a, o_b, o_c, w_out, mod4, norm_g, wgu, wd, final_g)


def _inproj_kernel(x_ref, mod_ref, ng_ref, w_ref, o_ref):
    h = _norm_mod(x_ref[...], ng_ref[1:2, :], mod_ref[3:4, :], mod_ref[4:5, :]).astype(BF16)
    o_ref[...] = _dot(h, w_ref[...])


def _inproj(x, mod4, norm_g, w_packed, layer):
    bsz, seq, _ = x.shape
    return pl.pallas_call(
        _inproj_kernel,
        out_shape=jax.ShapeDtypeStruct((bsz, seq, P_TOTAL), F32),
        grid=(bsz, seq // TOK_TILE),
        in_specs=[
            pl.BlockSpec((None, TOK_TILE, D_MODEL), lambda b, i: (b, i, 0)),
            pl.BlockSpec((None, None, 9, D_MODEL), lambda b, i: (layer, b, 0, 0)),
            pl.BlockSpec((None, 3, D_MODEL), lambda b, i: (layer, 0, 0)),
            _const_spec((D_MODEL, P_TOTAL)),
        ],
        out_specs=pl.BlockSpec((None, TOK_TILE, P_TOTAL), lambda b, i: (b, i, 0)),
        compiler_params=_params(("arbitrary", "arbitrary")),
        name="in_proj",
    )(x, mod4, norm_g, w_packed)


def _pack_w_in(w):
    z = lambda n: jnp.zeros((w.shape[0], n), w.dtype)
    cols = [
        w[:, 0:1024],
        w[:, 1408:1920],
        w[:, 1024:1280],
        w[:, 1280:1344],
        w[:, 1920:1984],
        w[:, 1344:1408],
        w[:, 1984:1992],
        z(56),
        w[:, 2504:3528],
        w[:, 1992:2504],
        w[:, 3528:3536],
        z(120),
    ]
    return jnp.concatenate(cols, axis=1).astype(BF16)


def _split3(x):
    hi = x.astype(BF16)
    r = x - hi.astype(F32)
    mid = r.astype(BF16)
    lo = (r - mid.astype(F32)).astype(BF16)
    return hi, mid, lo


def _hgrn2_kernel(blk_ref, lbl_ref, ng_ref, tril_ref, bones_ref, o_ref, st_ref, p_ref, *, layer):
    @pl.when(pl.program_id(1) == 0)
    def _():
        st_ref[...] = jnp.zeros_like(st_ref)

    lg = lbl_ref[...]
    e = jnp.exp(lg - jnp.max(lg, axis=0, keepdims=True))
    sm = e / jnp.sum(e, axis=0, keepdims=True)
    cs = sm[0:1, :]
    for i in range(1, layer + 1):
        cs = cs + sm[i:i + 1, :]
    lb = cs - sm[0:1, :]

    for s in range(blk_ref.shape[0]):
        _hgrn2_chunk(blk_ref.at[s], lb, ng_ref, tril_ref, bones_ref, o_ref.at[s], st_ref.at[s], p_ref.at[s])


def _hgrn2_chunk(blk_ref, lb, ng_ref, tril_ref, bones_ref, o_ref, st_ref, p_ref):
    C, W, SB = HG_CHUNK, HG_WIDTH, HG_SUB
    q = blk_ref[:, 0:W]
    fl = blk_ref[:, W:2 * W]
    iv = blk_ref[:, 2 * W:3 * W]
    g = blk_ref[:, 3 * W:4 * W]
    f = lb + (1.0 - lb) * _sigmoid(fl)
    logf = jnp.log(jnp.maximum(f, 1e-30))
    kk = (1.0 - lb) * _sigmoid(-fl)
    b = jnp.dot(tril_ref[...], logf, preferred_element_type=F32, precision=HIGHEST)

    lane = lax.broadcasted_iota(jnp.int32, (1, W), 1)
    hmask = [(lane >= h * HG_DIM) & (lane < (h + 1) * HG_DIM) for h in range(HG_HEADS)]
    bones = bones_ref[...]
    iv_bf = iv.astype(BF16)

    rmod = lax.broadcasted_iota(jnp.int32, (C, W), 0) & (SB - 1)
    for d in range(SB):
        if d == 0:
            p = q * kk
        else:
            valid = rmod >= d
            dec = jnp.exp(jnp.where(valid, b - pltpu.roll(b, d, 0), 0.0))
            p = jnp.where(valid, q * pltpu.roll(kk, d, 0) * dec, 0.0)
        p_ref[d * C:(d + 1) * C, :] = p.astype(BF16)
    wsum = _dot(p_ref[...], bones)
    o = wsum[0:C, :] * iv
    for d in range(1, SB):
        o = o + wsum[d * C:(d + 1) * C, :] * pltpu.roll(iv, d, 0)

    parts = [jnp.zeros((SB, W), F32)]
    for i_sub in range(1, C // SB):
        r0 = i_sub * SB
        bref = b[r0 - 1:r0, :]
        qp = q[r0:r0 + SB, :] * jnp.exp(b[r0:r0 + SB, :] - bref)
        kp = (kk[0:r0, :] * jnp.exp(bref - b[0:r0, :])).astype(BF16)
        qstk = jnp.concatenate([jnp.where(hmask[h], qp, 0.0) for h in range(HG_HEADS)], axis=0).astype(BF16)
        sc = _dot_nt(qstk, kp)
        oi = _dot(sc.astype(BF16), iv_bf[0:r0, :])
        acc = jnp.where(hmask[0], oi[0:SB, :], 0.0)
        for h in range(1, HG_HEADS):
            acc = acc + jnp.where(hmask[h], oi[h * SB:(h + 1) * SB, :], 0.0)
        parts.append(acc)
    o = o + jnp.concatenate(parts, axis=0)

    st = st_ref[...]
    o = o + _dot_nt((q * jnp.exp(b)).astype(BF16), st.astype(BF16))
    b_end = b[C - 1:C, :]
    kd = (kk * jnp.exp(b_end - b)).astype(BF16)
    upd = _dot_tn(iv_bf, kd)
    st_ref[...] = st * jnp.exp(b_end) + jnp.where(bones > 0, upd, 0.0)

    hi, mid, lo = _split3(o * o)
    ss = _dot(jnp.concatenate([hi, mid, lo], axis=0), bones)
    ms = (ss[0:C, :] + ss[C:2 * C, :] + ss[2 * C:3 * C, :]) * (1.0 / HG_DIM)
    o = (o * lax.rsqrt(ms + EPS)) * ng_ref[...]
    o_ref[...] = (o * (g * _sigmoid(g))).astype(BF16)


def _hgrn2(proj, lb_logits, norm_g, layer):
    bsz, seq, _ = proj.shape
    tril = jnp.tril(jnp.ones((HG_CHUNK, HG_CHUNK), F32))
    head = np.arange(HG_WIDTH) // HG_DIM
    bones = jnp.asarray(head[:, None] == head[None, :], BF16)
    kern = functools.partial(_hgrn2_kernel, layer=layer)
    return pl.pallas_call(
        kern,
        out_shape=jax.ShapeDtypeStruct((bsz, seq, HG_WIDTH), BF16),
        grid=(bsz // MIX_BATCH, seq // HG_CHUNK),
        in_specs=[
            pl.BlockSpec((MIX_BATCH, HG_CHUNK, 4 * HG_WIDTH), lambda b, c: (b, c, P_HG // (4 * HG_WIDTH))),
            pl.BlockSpec((DEPTH, HG_WIDTH), lambda b, c: (0, 0)),
            pl.BlockSpec((1, HG_WIDTH), lambda b, c: (0, 0)),
            pl.BlockSpec((HG_CHUNK, HG_CHUNK), lambda b, c: (0, 0)),
            pl.BlockSpec((HG_WIDTH, HG_WIDTH), lambda b, c: (0, 0)),
        ],
        out_specs=pl.BlockSpec((MIX_BATCH, HG_CHUNK, HG_WIDTH), lambda b, c: (b, c, 0)),
        scratch_shapes=[
            pltpu.VMEM((MIX_BATCH, HG_WIDTH, HG_WIDTH), F32),
            pltpu.VMEM((MIX_BATCH, HG_SUB * HG_CHUNK, HG_WIDTH), BF16),
        ],
        compiler_params=_params(("arbitrary", "arbitrary")),
        name="hgrn2",
    )(proj, lb_logits, norm_g[layer].reshape(1, HG_WIDTH), tril, bones)


_BIT_SWAP_MASK = {16: 0x0000FFFF, 8: 0x00FF00FF, 4: 0x0F0F0F0F, 2: 0x33333333, 1: 0x55555555}


def _dsa_kernel(iq_ref, aq_ref, kv_ref, cos_ref, sin_ref, lng_ref, lnb_ref, o_ref,
                kidx_ref, kr_ref, vt_ref, l2_ref, s2_ref, score_ref, plane_ref, p_ref, thr_ref, cnt_ref, sg_ref,
                *, n_keep, n_blocks):
    j = pl.program_id(1)
    QB = Q_BLOCK
    lane = lax.broadcasted_iota(jnp.int32, (QB, LANES), 1)
    lo64 = lane < 64
    first_half = (lane & 63) < 32
    cos = cos_ref[...]
    sin = sin_ref[...]

    def rope(x):
        rot = jnp.where(first_half, pltpu.roll(x, 96, 1), pltpu.roll(x, 32, 1))
        return x * cos + rot * sin

    def head_pair(x):
        return jnp.concatenate([jnp.where(lo64, x, 0.0), jnp.where(lo64, 0.0, x)], axis=0).astype(BF16)

    @pl.when(j == 0)
    def _():
        kidx_ref[...] = jnp.zeros_like(kidx_ref)
        kr_ref[...] = jnp.zeros_like(kr_ref)
        vt_ref[...] = jnp.zeros_like(vt_ref)

    a = kv_ref[:, 0:LANES]
    mu = jnp.sum(jnp.where(lo64, 0.0, a), axis=-1, keepdims=True) * (1.0 / IDX_DIM)
    xc = a - mu
    var = jnp.sum(jnp.where(lo64, 0.0, xc * xc), axis=-1, keepdims=True) * (1.0 / IDX_DIM)
    y = (xc * lax.rsqrt(var + EPS)) * lng_ref[...] + lnb_ref[...]
    kr = rope(jnp.where(lo64, a, y))
    krs = pltpu.roll(kr, 64, 1)
    kr_ref[j] = jnp.where(lo64, kr, krs).astype(BF16)
    kidx_ref[j] = jnp.where(lo64, krs, kr).astype(BF16)
    bvt = kv_ref[:, LANES:2 * LANES].T
    vt_ref[j] = bvt[0:ATT_DIM, :].astype(BF16)
    w_t = bvt[ATT_DIM:ATT_DIM + IDX_HEADS, :] * (IDX_HEADS ** -0.5 * IDX_DIM ** -0.5)

    def block(nb, j_lo, j_hi):
        nk = nb * QB
        nt = nk // DSA_ROWS
        row_iota = lax.broadcasted_iota(jnp.int32, (DSA_ROWS, QB), 0)
        tpos = j * QB + lax.broadcasted_iota(jnp.int32, (DSA_ROWS, QB), 1)

        def rows(i):
            return pl.ds(pl.multiple_of(i * DSA_ROWS, DSA_ROWS), DSA_ROWS)

        def fold(x, op):
            y = x.reshape(DSA_ROWS // 8, 8, QB)
            acc = y[0]
            for r in range(1, DSA_ROWS // 8):
                acc = op(acc, y[r])
            return acc

        kidx = kidx_ref[0:nb].reshape(nk, LANES)
        for m in range(IDX_HEADS // 2):
            qr = rope(iq_ref[:, m * LANES:(m + 1) * LANES])
            l2_ref[m, 0:nk, :] = _dot_nt(kidx, head_pair(qr))

        def score_tile(i, carry):
            t = None
            for m in range(IDX_HEADS // 2):
                u = (jnp.maximum(l2_ref[m, rows(i), 0:QB], 0.0) * w_t[2 * m:2 * m + 1, :]
                     + jnp.maximum(l2_ref[m, rows(i), QB:2 * QB], 0.0) * w_t[2 * m + 1:2 * m + 2, :])
                t = u if t is None else t + u
            t = jnp.where(t == 0.0, 0.0, t)
            score_ref[rows(i), :] = jnp.where(i * DSA_ROWS + row_iota <= tpos, t, -jnp.inf)
            return carry

        lax.fori_loop(0, nt, score_tile, 0)

        def count(pred_fn):
            acc = jnp.zeros((COUNT_ROWS, QB), F32)
            for r0 in range(0, nk, COUNT_ROWS):
                acc = acc + jnp.where(pred_fn(score_ref[r0:r0 + COUNT_ROWS, :], r0), 1.0, 0.0)
            return jnp.sum(acc, axis=0, keepdims=True)

        def ordered_bits(v):
            return v ^ ((v >> 31) & jnp.int32(0x7FFFFFFF))

        def key_value(k):
            return lax.bitcast_convert_type(ordered_bits(k), F32)

        def attention_logits():
            krd = kr_ref[0:nb].reshape(nk, LANES)
            for m in range(ATT_HEADS // 2):
                qr = rope(aq_ref[:, m * LANES:(m + 1) * LANES]) * (ATT_DIM ** -0.5)
                s2_ref[m, 0:nk, :] = _dot_nt(krd, head_pair(qr))

        def keep_all():
            thr_ref[...] = jnp.full((1, QB), -jnp.inf, F32)
            sg_ref[...] = jnp.full((1, QB), -1, jnp.int32)

        def search(with_attention_logits):
            n_groups = nk // (32 * 8)

            def plane_group(g, carry):
                base = pl.multiple_of(g * 256, 256)
                x = [ordered_bits(lax.bitcast_convert_type(score_ref[pl.ds(base + 8 * i, 8), :], jnp.int32))
                     ^ jnp.int32(INT_MIN) for i in range(32)]
                step = 16
                while step:
                    msk = jnp.int32(_BIT_SWAP_MASK[step])
                    for k in range(32):
                        if k & step == 0:
                            a, b = x[k], x[k | step]
                            t = (lax.shift_right_logical(a, jnp.int32(step)) ^ b) & msk
                            x[k | step] = b ^ t
                            x[k] = a ^ lax.shift_left(t, jnp.int32(step))
                    step >>= 1
                for b in range(32):
                    plane_ref[b, pl.ds(pl.multiple_of(g * 8, 8), 8), :] = x[b]
                return carry

            lax.fori_loop(0, n_groups, plane_group, 0)

            def popsum(words):
                pc = lax.population_count(words).reshape(n_groups, 8, QB)
                acc = pc[0]
                for g in range(1, n_groups):
                    acc = acc + pc[g]
                return jnp.sum(acc.astype(F32), axis=0, keepdims=True).astype(jnp.int32)

            def body(i, carry):
                active, need, thr = carry
                hit = active & plane_ref[31 - i, 0:8 * n_groups, :]
                ones = popsum(hit)
                take = ones >= need
                active = jnp.where(take, hit, active ^ hit)
                need = jnp.where(take, need, need - ones)
                thr = jnp.where(take, thr | lax.shift_left(jnp.int32(1), 31 - i), thr)
                return active, need, thr

            _, _, thr_key = lax.fori_loop(
                0, 32, body,
                (jnp.full((8 * n_groups, QB), -1, jnp.int32), jnp.full((1, QB), n_keep, jnp.int32),
                 jnp.zeros((1, QB), jnp.int32)))
            thr = key_value(thr_key ^ jnp.int32(INT_MIN))
            if with_attention_logits:
                attention_logits()

            def rank_counts(t):
                return count(lambda s, r0: s > t), count(lambda s, r0: s >= t)

            n_gt, n_ge = rank_counts(thr)
            thr_ref[...] = thr
            cnt_ref[0:1, :] = n_gt
            cnt_ref[1:2, :] = n_ge
            wrong = jnp.max(jnp.where(n_gt < n_keep, jnp.where(n_ge >= n_keep, 0.0, 1.0), 1.0))

            @pl.when(wrong > 0.5)
            def _():
                n_pos = count(lambda s, r0: s >= 0.0)
                lo0 = jnp.where(n_pos >= n_keep, jnp.int32(0), jnp.int32(INT_MIN))

                def bisect(i, lo):
                    cand = lo | lax.shift_left(jnp.int32(1), 30 - i)
                    t = key_value(cand)
                    return jnp.where(count(lambda s, r0: s >= t) >= n_keep, cand, lo)

                t = key_value(lax.fori_loop(0, 31, bisect, lo0))
                n_gt2, n_ge2 = rank_counts(t)
                thr_ref[...] = t
                cnt_ref[0:1, :] = n_gt2
                cnt_ref[1:2, :] = n_ge2

            thr = thr_ref[...]
            need = n_keep - cnt_ref[0:1, :]
            n_eq = cnt_ref[1:2, :] - cnt_ref[0:1, :]
            sg_ref[...] = jnp.full((1, QB), nk, jnp.int32)

            @pl.when(jnp.max(n_eq - need) > 0.5)
            def _():
                crow = lax.broadcasted_iota(jnp.int32, (COUNT_ROWS, QB), 0)

                def body2(i, sg):
                    cand = sg | lax.shift_left(jnp.int32(1), 10 - i)
                    cnt = count(lambda s, r0: jnp.where(r0 + crow < cand, s, -jnp.inf) == thr)
                    return jnp.where(cnt < need, cand, sg)

                sg_ref[...] = lax.fori_loop(0, 11, body2, jnp.zeros((1, QB), jnp.int32))

        if (j_hi - 1) * QB < n_keep:
            keep_all()
            attention_logits()
        elif j_lo * QB >= n_keep:
            search(True)
        else:
            pl.when(j * QB < n_keep)(keep_all)
            pl.when(j * QB >= n_keep)(functools.partial(search, False))
            attention_logits()
        thr = thr_ref[...]
        sg = sg_ref[...]

        def head_cols(h):
            return h // 2, slice((h % 2) * QB, (h % 2 + 1) * QB)

        def mask_tile(i, mx):
            sc = score_ref[rows(i), :]
            tie = jnp.where(i * DSA_ROWS + row_iota <= sg, jnp.inf, NEG_BIG)
            bias = jnp.where(sc > thr, jnp.inf, jnp.where(sc == thr, tie, NEG_BIG))
            out = []
            for h in range(ATT_HEADS):
                m, cols = head_cols(h)
                s = jnp.minimum(s2_ref[m, rows(i), cols], bias)
                s2_ref[m, rows(i), cols] = s
                out.append(jnp.maximum(mx[h], fold(s, jnp.maximum)))
            return tuple(out)

        mx = lax.fori_loop(0, nt, mask_tile, tuple(jnp.full((8, QB), -jnp.inf, F32) for _ in range(ATT_HEADS)))
        mx = [jnp.max(v, axis=0, keepdims=True) for v in mx]

        def prob_tile(i, ls):
            out = []
            for h in range(ATT_HEADS):
                m, cols = head_cols(h)
                p = jnp.exp(s2_ref[m, rows(i), cols] - mx[h])
                p_ref[rows(i), h * QB:(h + 1) * QB] = p.astype(BF16)
                out.append(ls[h] + fold(p, jnp.add))
            return tuple(out)

        ls = lax.fori_loop(0, nt, prob_tile, tuple(jnp.zeros((8, QB), F32) for _ in range(ATT_HEADS)))
        vt = jnp.concatenate([vt_ref[i] for i in range(nb)], axis=1)
        o4 = _dot(vt, p_ref[0:nk, :])
        outs = [o4[:, h * QB:(h + 1) * QB] * (1.0 / jnp.sum(ls[h], axis=0, keepdims=True))
                for h in range(ATT_HEADS)]
        o_ref[...] = jnp.concatenate(outs, axis=0).T.astype(BF16)

    per_class = n_blocks // DSA_CLASSES
    for cls in range(DSA_CLASSES):
        @pl.when((j >= cls * per_class) & (j < (cls + 1) * per_class))
        def _(cls=cls):
            block((cls + 1) * per_class, cls * per_class, (cls + 1) * per_class)


def _rope_tables(seq):
    inv_freq = 1.0 / (ROPE_THETA ** (jnp.arange(0, ATT_DIM, 2, dtype=F32) / ATT_DIM))
    ang = jnp.arange(seq, dtype=F32)[:, None] * inv_freq[None, :]
    cos, sin = jnp.cos(ang), jnp.sin(ang)
    return jnp.concatenate([cos] * 4, axis=1), jnp.concatenate([-sin, sin, -sin, sin], axis=1)


def _dsa(proj, cos, sin, kn_g, kn_b, layer):
    bsz, seq, _ = proj.shape
    n_blocks = seq // Q_BLOCK
    n_keep = min(TOPK_MAX, seq // 4)
    pad = jnp.zeros((IDX_DIM,), F32)
    lng = jnp.concatenate([pad, kn_g[layer]]).reshape(1, LANES)
    lnb = jnp.concatenate([pad, kn_b[layer]]).reshape(1, LANES)
    kern = functools.partial(_dsa_kernel, n_keep=n_keep, n_blocks=n_blocks)
    col = lambda width, off: pl.BlockSpec((None, Q_BLOCK, width), lambda b, j: (b, j, off // width))
    return pl.pallas_call(
        kern,
        out_shape=jax.ShapeDtypeStruct((bsz, seq, ATT_WIDTH), BF16),
        grid=(bsz, n_blocks),
        in_specs=[
            col(IDX_HEADS * IDX_DIM, P_IQ), col(ATT_WIDTH, P_AQ), col(2 * LANES, P_KV),
            pl.BlockSpec((Q_BLOCK, LANES), lambda b, j: (j, 0)),
            pl.BlockSpec((Q_BLOCK, LANES), lambda b, j: (j, 0)),
            pl.BlockSpec((1, LANES), lambda b, j: (0, 0)),
            pl.BlockSpec((1, LANES), lambda b, j: (0, 0)),
        ],
        out_specs=pl.BlockSpec((None, Q_BLOCK, ATT_WIDTH), lambda b, j: (b, j, 0)),
        scratch_shapes=[
            pltpu.VMEM((n_blocks, Q_BLOCK, LANES), BF16),
            pltpu.VMEM((n_blocks, Q_BLOCK, LANES), BF16),
            pltpu.VMEM((n_blocks, ATT_DIM, Q_BLOCK), BF16),
            pltpu.VMEM((IDX_HEADS // 2, seq, 2 * Q_BLOCK), F32),
            pltpu.VMEM((ATT_HEADS // 2, seq, 2 * Q_BLOCK), F32),
            pltpu.VMEM((seq, Q_BLOCK), F32),
            pltpu.VMEM((32, seq // 32, Q_BLOCK), jnp.int32),
            pltpu.VMEM((seq, ATT_HEADS * Q_BLOCK), BF16),
            pltpu.VMEM((1, Q_BLOCK), F32),
            pltpu.VMEM((2, Q_BLOCK), F32),
            pltpu.VMEM((1, Q_BLOCK), jnp.int32),
        ],
        compiler_params=_params(("arbitrary", "arbitrary")),
        name="dsa",
    )(proj, proj, proj, cos, sin, lng, lnb)


def _ssd_kernel(xbc_ref, z_ref, dt_ref, cw_ref, cb_ref, dtb_ref, alog_ref, dsk_ref, ng_ref, tril_ref,
                o_ref, xpad_ref, st_ref):
    @pl.when(pl.program_id(1) == 0)
    def _():
        xpad_ref[...] = jnp.zeros_like(xpad_ref)
        st_ref[...] = jnp.zeros_like(st_ref)

    for s in range(xbc_ref.shape[0]):
        _ssd_chunk(xbc_ref.at[s], z_ref.at[s], dt_ref.at[s], cw_ref, cb_ref, dtb_ref, alog_ref, dsk_ref, ng_ref,
                   tril_ref, o_ref.at[s], xpad_ref.at[s], st_ref.at[s])


def _ssd_chunk(xbc_ref, z_ref, dt_ref, cw_ref, cb_ref, dtb_ref, alog_ref, dsk_ref, ng_ref, tril_ref,
               o_ref, xpad_ref, st_ref):
    L = SSM_CHUNK
    HW = SSM_WIDTH // SSM_GROUPS

    x = xbc_ref[...]
    conv = cb_ref[...] + cw_ref[SSM_CONV - 1:SSM_CONV, :] * x
    for d in range(1, SSM_CONV):
        conv = conv + cw_ref[SSM_CONV - 1 - d:SSM_CONV - d, :] * pltpu.roll(x, d, 0)
    hx = jnp.concatenate([xpad_ref[...], x[0:8, :]], axis=0)
    head = cb_ref[...]
    for i in range(SSM_CONV):
        off = 8 - (SSM_CONV - 1) + i
        head = head + cw_ref[i:i + 1, :] * hx[off:off + 8, :]
    conv = jnp.concatenate([head, conv[8:, :]], axis=0)
    xpad_ref[...] = x[L - 8:L, :]
    xc = conv * _sigmoid(conv)
    xs = xc[:, 0:SSM_WIDTH]

    dtr = dt_ref[...] + dtb_ref[...]
    dt128 = jnp.maximum(dtr, 0.0) + jnp.log(1.0 + jnp.exp(-jnp.abs(dtr)))
    da128 = dt128 * (-jnp.exp(alog_ref[...]))
    acs128 = jnp.dot(tril_ref[...], da128, preferred_element_type=F32, precision=HIGHEST)
    acs_t = acs128.T

    lane = lax.broadcasted_iota(jnp.int32, (L, LANES), 1)
    lo64 = lane < 64

    def col(x, h):
        return jnp.broadcast_to(x[:, h:h + 1], (L, LANES))

    def wide(x):
        return jnp.concatenate([jnp.where(lo64, col(x, 2 * m), col(x, 2 * m + 1)) for m in range(SSM_HEADS // 2)],
                               axis=1)

    dt_w = wide(dt128)
    acs_w = wide(acs128)
    aend_w = acs_w[L - 1:L, :]
    xdt = xs * dt_w
    xdt_dec = (xdt * jnp.exp(aend_w - acs_w)).astype(BF16)
    xdt_bf = xdt.astype(BF16)
    eacs_w = jnp.exp(acs_w)
    eaend_w = jnp.exp(aend_w)

    tri = lax.broadcasted_iota(jnp.int32, (L, L), 1) <= lax.broadcasted_iota(jnp.int32, (L, L), 0)

    def decay(h):
        diff = col(acs128, h) - jnp.broadcast_to(acs_t[h:h + 1, :], (L, L))
        return jnp.where(tri, jnp.exp(jnp.minimum(diff, 0.0)), 0.0)

    ys = []
    for g in range(SSM_GROUPS):
        bm = xc[:, SSM_WIDTH + g * SSM_STATE:SSM_WIDTH + (g + 1) * SSM_STATE]
        cm = xc[:, SSM_WIDTH + (SSM_GROUPS + g) * SSM_STATE:SSM_WIDTH + (SSM_GROUPS + g + 1) * SSM_STATE]
        cm_bf = cm.astype(BF16)
        cb = _dot_nt(cm_bf, bm.astype(BF16))
        st_g = st_ref[:, g * HW:(g + 1) * HW]
        y_off = _dot(cm_bf, st_g.astype(BF16)) * eacs_w[:, g * HW:(g + 1) * HW]
        new_st = _dot(bm.T.astype(BF16), xdt_dec[:, g * HW:(g + 1) * HW])
        st_ref[:, g * HW:(g + 1) * HW] = st_g * eaend_w[:, g * HW:(g + 1) * HW] + new_st
        slabs = []
        for mm in range(HW // LANES):
            m = g * (HW // LANES) + mm
            xslab = xdt_bf[:, m * LANES:(m + 1) * LANES]
            y0 = _dot((cb * decay(2 * m)).astype(BF16), xslab)
            y1 = _dot((cb * decay(2 * m + 1)).astype(BF16), xslab)
            slabs.append(jnp.where(lo64, y0, y1))
        ys.append(jnp.concatenate(slabs, axis=1) + y_off)
    y = jnp.concatenate(ys, axis=1) + xs * dsk_ref[...]
    z = z_ref[...]
    y = y * (z * _sigmoid(z))
    outs = []
    for g in range(SSM_GROUPS):
        yg = y[:, g * HW:(g + 1) * HW]
        yn = yg * lax.rsqrt(jnp.mean(yg * yg, axis=-1, keepdims=True) + EPS)
        outs.append(yn * ng_ref[:, g * HW:(g + 1) * HW])
    o_ref[...] = jnp.concatenate(outs, axis=1).astype(BF16)


def _ssd(proj, conv_w, conv_b, dt_bias, a_log, d_skip, norm_g):
    bsz, seq, _ = proj.shape
    pad = jnp.zeros((LANES - SSM_HEADS,), F32)
    dtb = jnp.concatenate([dt_bias, pad]).reshape(1, LANES)
    alog = jnp.concatenate([a_log, pad]).reshape(1, LANES)
    dsk = jnp.repeat(d_skip, SSM_HEADDIM).reshape(1, SSM_WIDTH)
    tril = jnp.tril(jnp.ones((SSM_CHUNK, SSM_CHUNK), F32))
    col = lambda width, off: pl.BlockSpec((MIX_BATCH, SSM_CHUNK, width), lambda b, c: (b, c, off // width))
    full = lambda r, w: pl.BlockSpec((r, w), lambda b, c: (0, 0))
    return pl.pallas_call(
        _ssd_kernel,
        out_shape=jax.ShapeDtypeStruct((bsz, seq, SSM_WIDTH), BF16),
        grid=(bsz // MIX_BATCH, seq // SSM_CHUNK),
        in_specs=[
            col(CONV_CH, P_XBC), col(SSM_WIDTH, P_Z), col(LANES, P_DT),
            full(SSM_CONV, CONV_CH), full(1, CONV_CH), full(1, LANES), full(1, LANES),
            full(1, SSM_WIDTH), full(1, SSM_WIDTH), full(SSM_CHUNK, SSM_CHUNK),
        ],
        out_specs=pl.BlockSpec((MIX_BATCH, SSM_CHUNK, SSM_WIDTH), lambda b, c: (b, c, 0)),
        scratch_shapes=[
            pltpu.VMEM((MIX_BATCH, 8, CONV_CH), F32),
            pltpu.VMEM((MIX_BATCH, SSM_STATE, SSM_WIDTH), F32),
        ],
        compiler_params=_params(("arbitrary", "arbitrary")),
        name="ssd",
    )(proj, proj, proj, conv_w, conv_b.reshape(1, CONV_CH), dtb, alog, dsk, norm_g.reshape(1, SSM_WIDTH), tril)


def kernel(x, c, w_ada, b_ada, norm_g, w_ffn_gu, w_ffn_down, w_in, w_out, lb_logits, hg_norm_g, idx_k_norm_g,
           idx_k_norm_b, conv_w, conv_b, dt_bias, a_log, d_skip, ssm_norm_g, final_norm_g):
    bsz, seq, _ = x.shape
    mod4 = _ada(c, w_ada, b_ada).reshape(DEPTH, bsz, 9, D_MODEL)
    cos, sin = _rope_tables(seq)
    fg = final_norm_g.reshape(1, D_MODEL)
    for l in range(DEPTH):
        x = _ffn(x, mod4, norm_g, w_ffn_gu[l, 0].astype(BF16), w_ffn_down[l, 0].astype(BF16), fg, l, 0, False)
        proj = _inproj(x, mod4, norm_g, _pack_w_in(w_in[l]), l)
        o_a = _hgrn2(proj, lb_logits, hg_norm_g, l)
        o_b = _dsa(proj, cos, sin, idx_k_norm_g, idx_k_norm_b, l)
        o_c = _ssd(proj, conv_w[l], conv_b[l], dt_bias[l], a_log[l], d_skip[l], ssm_norm_g[l])
        x = _mix_ffn(x, o_a, o_b, o_c, w_out[l].astype(BF16), mod4, norm_g, w_ffn_gu[l, 1].astype(BF16),
                     w_ffn_down[l, 1].astype(BF16), fg, l, l == DEPTH - 1)
    return x
```

```python
import functools

import numpy as np
import jax
import jax.numpy as jnp
from jax import lax
from jax.experimental import pallas as pl
from jax.experimental.pallas import tpu as pltpu

F32 = jnp.float32
BF16 = jnp.bfloat16
HIGHEST = lax.Precision.HIGHEST

D_MODEL = 1024
DEPTH = 2
HG_HEADS = 4
HG_DIM = 64
HG_WIDTH = HG_HEADS * HG_DIM
HG_CHUNK = 64
HG_SUB = 16
assert HG_SUB == 16
HG_P_ROWS = 8 * HG_CHUNK + 8 * (HG_CHUNK // 2)
ATT_HEADS = 4
ATT_DIM = 64
ATT_WIDTH = ATT_HEADS * ATT_DIM
IDX_HEADS = 8
IDX_DIM = 64
TOPK_MAX = 256
Q_BLOCK = 128
ROPE_THETA = 10000.0
SSM_HEADS = 8
SSM_HEADDIM = 64
SSM_WIDTH = SSM_HEADS * SSM_HEADDIM
SSM_GROUPS = 2
SSM_STATE = 128
SSM_CONV = 4
SSM_CHUNK = 128
CONV_CH = SSM_WIDTH + 2 * SSM_GROUPS * SSM_STATE
MIX_WIDTH = HG_WIDTH + ATT_WIDTH + SSM_WIDTH
D_FF = 2816
EPS = 1e-6
NEG_BIG = -1e30
INT_MIN = -(2 ** 31)

LANES = 128
VMEM_LIMIT_BYTES = 56 * 1024 * 1024

P_HG = 0
P_IQ = 1024
P_AQ = 1536
P_KV = 1792
P_XBC = 2048
P_Z = 3072
P_DT = 3584
P_TOTAL = 3712

MIX_BATCH = 4
DSA_CLASSES = 8
DSA_PART = 512
DSA_ROWS = 128
COUNT_ROWS = 64
TOK_TILE = 1024
FFN_TILE = 1024
FFN_CHUNK = 256


def _sigmoid(x):
    return 1.0 / (1.0 + jnp.exp(-x))


def _norm_mod(x, g, shift, scale):
    y = x * lax.rsqrt(jnp.mean(x * x, axis=-1, keepdims=True) + EPS)
    return (y * g) * (1.0 + scale) + shift


def _dot(a, b):
    return jnp.dot(a, b, preferred_element_type=F32)


def _dot_nt(a, b):
    return lax.dot_general(a, b, (((1,), (1,)), ((), ())), preferred_element_type=F32)


def _dot_tn(a, b):
    return lax.dot_general(a, b, (((0,), (0,)), ((), ())), preferred_element_type=F32)


def _params(sem):
    return pltpu.CompilerParams(dimension_semantics=sem, vmem_limit_bytes=VMEM_LIMIT_BYTES)


def _const_spec(shape):
    nd = len(shape)
    return pl.BlockSpec(shape, lambda *_: (0,) * nd, pipeline_mode=pl.Buffered(1))


def _ada_kernel(c_ref, w_ref, b_ref, o_ref):
    c = c_ref[...]
    cond = c * _sigmoid(c)
    o_ref[...] = jnp.dot(cond, w_ref[...], preferred_element_type=F32, precision=HIGHEST) + b_ref[...]


def _ada(c, w_ada, b_ada):
    bsz = c.shape[0]
    n_mod = w_ada.shape[-1]
    nt = n_mod // D_MODEL
    return pl.pallas_call(
        _ada_kernel,
        out_shape=jax.ShapeDtypeStruct((DEPTH, bsz, n_mod), F32),
        grid=(DEPTH, nt),
        in_specs=[
            pl.BlockSpec((bsz, D_MODEL), lambda l, n: (0, 0)),
            pl.BlockSpec((None, D_MODEL, D_MODEL), lambda l, n: (l, 0, n)),
            pl.BlockSpec((None, 1, D_MODEL), lambda l, n: (l, 0, n)),
        ],
        out_specs=pl.BlockSpec((None, bsz, D_MODEL), lambda l, n: (l, 0, n)),
        compiler_params=_params(("arbitrary", "arbitrary")),
        name="ada_mod",
    )(c, w_ada, b_ada.reshape(DEPTH, 1, n_mod))


def _ffn_kernel(x_ref, mod_ref, ng_ref, wgu_ref, wd_ref, fg_ref, o_ref, act_ref, *, sub, final):
    _ffn_body(x_ref[...], mod_ref, ng_ref, wgu_ref, wd_ref, fg_ref, o_ref, act_ref, sub, final)


def _mix_ffn_kernel(x_ref, oa_ref, ob_ref, oc_ref, wo_ref, mod_ref, ng_ref, wgu_ref, wd_ref, fg_ref, o_ref, act_ref,
                    *, final):
    acc = _dot(oa_ref[...], wo_ref[0:HG_WIDTH, :])
    acc += _dot(ob_ref[...], wo_ref[HG_WIDTH:HG_WIDTH + ATT_WIDTH, :])
    acc += _dot(oc_ref[...], wo_ref[HG_WIDTH + ATT_WIDTH:MIX_WIDTH, :])
    x = x_ref[...] + mod_ref[5:6, :] * acc
    _ffn_body(x, mod_ref, ng_ref, wgu_ref, wd_ref, fg_ref, o_ref, act_ref, 2, final)


def _ffn_body(x, mod_ref, ng_ref, wgu_ref, wd_ref, fg_ref, o_ref, act_ref, sub, final):
    h = _norm_mod(x, ng_ref[sub:sub + 1, :], mod_ref[3 * sub:3 * sub + 1, :], mod_ref[3 * sub + 1:3 * sub + 2, :])
    h = h.astype(BF16)
    for c0 in range(0, D_FF, FFN_CHUNK):
        gate = _dot(h, wgu_ref[:, c0:c0 + FFN_CHUNK])
        up = _dot(h, wgu_ref[:, D_FF + c0:D_FF + c0 + FFN_CHUNK])
        act_ref[:, c0:c0 + FFN_CHUNK] = (gate * _sigmoid(gate) * up).astype(BF16)
    out = _dot(act_ref[...], wd_ref[...])
    y = x + (0.5 * mod_ref[3 * sub + 2:3 * sub + 3, :]) * out
    if final:
        y = (y * lax.rsqrt(jnp.mean(y * y, axis=-1, keepdims=True) + EPS)) * fg_ref[...]
    o_ref[...] = y


def _ffn(x, mod4, norm_g, wgu, wd, final_g, layer, sub, final):
    bsz, seq, _ = x.shape
    kern = functools.partial(_ffn_kernel, sub=sub, final=final)
    return pl.pallas_call(
        kern,
        out_shape=jax.ShapeDtypeStruct(x.shape, F32),
        grid=(bsz, seq // FFN_TILE),
        in_specs=[
            pl.BlockSpec((None, FFN_TILE, D_MODEL), lambda b, i: (b, i, 0)),
            pl.BlockSpec((None, None, 9, D_MODEL), lambda b, i: (layer, b, 0, 0)),
            pl.BlockSpec((None, 3, D_MODEL), lambda b, i: (layer, 0, 0)),
            _const_spec((D_MODEL, 2 * D_FF)),
            _const_spec((D_FF, D_MODEL)),
            _const_spec((1, D_MODEL)),
        ],
        out_specs=pl.BlockSpec((None, FFN_TILE, D_MODEL), lambda b, i: (b, i, 0)),
        scratch_shapes=[pltpu.VMEM((FFN_TILE, D_FF), BF16)],
        compiler_params=_params(("arbitrary", "arbitrary")),
        name="ffn",
    )(x, mod4, norm_g, wgu, wd, final_g)


def _mix_ffn(x, o_a, o_b, o_c, w_out, mod4, norm_g, wgu, wd, final_g, layer, final):
    bsz, seq, _ = x.shape
    tile = lambda w: pl.BlockSpec((None, FFN_TILE, w), lambda b, i: (b, i, 0))
    return pl.pallas_call(
        functools.partial(_mix_ffn_kernel, final=final),
        out_shape=jax.ShapeDtypeStruct(x.shape, F32),
        grid=(bsz, seq // FFN_TILE),
        in_specs=[
            tile(D_MODEL), tile(HG_WIDTH), tile(ATT_WIDTH), tile(SSM_WIDTH),
            _const_spec((MIX_WIDTH, D_MODEL)),
            pl.BlockSpec((None, None, 9, D_MODEL), lambda b, i: (layer, b, 0, 0)),
            pl.BlockSpec((None, 3, D_MODEL), lambda b, i: (layer, 0, 0)),
            _const_spec((D_MODEL, 2 * D_FF)),
            _const_spec((D_FF, D_MODEL)),
            _const_spec((1, D_MODEL)),
        ],
        out_specs=tile(D_MODEL),
        scratch_shapes=[pltpu.VMEM((FFN_TILE, D_FF), BF16)],
        compiler_params=_params(("arbitrary", "arbitrary")),
        name="mix_ffn",
    )(x, o_a, o_b, o_c, w_out, mod4, norm_g, wgu, wd, final_g)


def _inproj_kernel(x_ref, mod_ref, ng_ref, w_ref, o_ref):
    h = _norm_mod(x_ref[...], ng_ref[1:2, :], mod_ref[3:4, :], mod_ref[4:5, :]).astype(BF16)
    o_ref[...] = _dot(h, w_ref[...])


def _inproj(x, mod4, norm_g, w_packed, layer):
    bsz, seq, _ = x.shape
    return pl.pallas_call(
        _inproj_kernel,
        out_shape=jax.ShapeDtypeStruct((bsz, seq, P_TOTAL), F32),
        grid=(bsz, seq // TOK_TILE),
        in_specs=[
            pl.BlockSpec((None, TOK_TILE, D_MODEL), lambda b, i: (b, i, 0)),
            pl.BlockSpec((None, None, 9, D_MODEL), lambda b, i: (layer, b, 0, 0)),
            pl.BlockSpec((None, 3, D_MODEL), lambda b, i: (layer, 0, 0)),
            _const_spec((D_MODEL, P_TOTAL)),
        ],
        out_specs=pl.BlockSpec((None, TOK_TILE, P_TOTAL), lambda b, i: (b, i, 0)),
        compiler_params=_params(("arbitrary", "arbitrary")),
        name="in_proj",
    )(x, mod4, norm_g, w_packed)


def _pack_w_in(w):
    z = lambda n: jnp.zeros((w.shape[0], n), w.dtype)
    cols = [
        w[:, 0:1024],
        w[:, 1408:1920],
        w[:, 1024:1280],
        w[:, 1280:1344],
        w[:, 1920:1984],
        w[:, 1344:1408],
        w[:, 1984:1992],
        z(56),
        w[:, 2504:3528],
        w[:, 1992:2504],
        w[:, 3528:3536],
        z(120),
    ]
    return jnp.concatenate(cols, axis=1).astype(BF16)


def _split3(x):
    hi = x.astype(BF16)
    r = x - hi.astype(F32)
    mid = r.astype(BF16)
    lo = (r - mid.astype(F32)).astype(BF16)
    return hi, mid, lo


def _hgrn2_kernel(blk_ref, lbl_ref, ng_ref, tril_ref, bones_ref, o_ref, st_ref, p_ref, *, layer):
    @pl.when(pl.program_id(1) == 0)
    def _():
        st_ref[...] = jnp.zeros_like(st_ref)

    lg = lbl_ref[...]
    e = jnp.exp(lg - jnp.max(lg, axis=0, keepdims=True))
    sm = e / jnp.sum(e, axis=0, keepdims=True)
    cs = sm[0:1, :]
    for i in range(1, layer + 1):
        cs = cs + sm[i:i + 1, :]
    lb = cs - sm[0:1, :]

    for s in range(blk_ref.shape[0]):
        _hgrn2_chunk(blk_ref.at[s], lb, ng_ref, tril_ref, bones_ref, o_ref.at[s], st_ref.at[s], p_ref.at[s])


def _hgrn2_chunk(blk_ref, lb, ng_ref, tril_ref, bones_ref, o_ref, st_ref, p_ref):
    C, W, SB = HG_CHUNK, HG_WIDTH, HG_SUB
    q = blk_ref[:, 0:W]
    fl = blk_ref[:, W:2 * W]
    iv = blk_ref[:, 2 * W:3 * W]
    g = blk_ref[:, 3 * W:4 * W]
    f = lb + (1.0 - lb) * _sigmoid(fl)
    logf = jnp.log(jnp.maximum(f, 1e-30))
    kk = (1.0 - lb) * _sigmoid(-fl)
    b = jnp.dot(tril_ref[...], logf, preferred_element_type=F32, precision=HIGHEST)

    lane = lax.broadcasted_iota(jnp.int32, (1, W), 1)
    hmask = [(lane >= h * HG_DIM) & (lane < (h + 1) * HG_DIM) for h in range(HG_HEADS)]
    bones = bones_ref[...]
    iv_bf = iv.astype(BF16)

    n_sub = C // SB
    H = n_sub * 8
    row8 = lax.broadcasted_iota(jnp.int32, (H, W), 0) & 7

    def halves(x):
        return (jnp.concatenate([x[i * SB:i * SB + 8, :] for i in range(n_sub)], axis=0),
                jnp.concatenate([x[i * SB + 8:(i + 1) * SB, :] for i in range(n_sub)], axis=0))

    def rot(x, d):
        if d == 0:
            return x
        return jnp.concatenate([pltpu.roll(x[8 * i:8 * i + 8, :], d, 0) for i in range(n_sub)], axis=0)

    def earlier(lo, hi, d):
        if d < 8:
            rl = rot(lo, d)
            return rl, jnp.where(row8 >= d, rot(hi, d), rl)
        return None, rot(lo, d - 8)

    q_lo, q_hi = halves(q)
    b_lo, b_hi = halves(b)
    kk_lo, kk_hi = halves(kk)
    iv_lo, iv_hi = halves(iv)

    def p_rows(d):
        if d < 8:
            return (C * d, C * d + H), (C * d + H, C * d + 2 * H)
        return None, (C * 8 + H * (d - 8), C * 8 + H * (d - 7))

    p_ref[0:C, :] = (q * kk).astype(BF16)
    for d in range(1, SB):
        sb_lo, sb_hi = earlier(b_lo, b_hi, d)
        sk_lo, sk_hi = earlier(kk_lo, kk_hi, d)
        r_lo, r_hi = p_rows(d)
        p_hi = q_hi * sk_hi * jnp.exp(b_hi - sb_hi)
        if d > 8:
            p_hi = jnp.where(row8 >= d - 8, p_hi, 0.0)
        p_ref[r_hi[0]:r_hi[1], :] = p_hi.astype(BF16)
        if d < 8:
            p_lo = jnp.where(row8 >= d, q_lo * sk_lo * jnp.exp(b_lo - sb_lo), 0.0)
            p_ref[r_lo[0]:r_lo[1], :] = p_lo.astype(BF16)
    wsum = _dot(p_ref[...], bones)
    o_lo = jnp.zeros((H, W), F32)
    o_hi = jnp.zeros((H, W), F32)
    for d in range(1, SB):
        si_lo, si_hi = earlier(iv_lo, iv_hi, d)
        r_lo, r_hi = p_rows(d)
        o_hi = o_hi + wsum[r_hi[0]:r_hi[1], :] * si_hi
        if d < 8:
            o_lo = o_lo + wsum[r_lo[0]:r_lo[1], :] * si_lo
    pieces = []
    for i in range(n_sub):
        pieces += [o_lo[8 * i:8 * i + 8, :], o_hi[8 * i:8 * i + 8, :]]
    o = wsum[0:C, :] * iv + jnp.concatenate(pieces, axis=0)

    parts = [jnp.zeros((SB, W), F32)]
    for i_sub in range(1, C // SB):
        r0 = i_sub * SB
        bref = b[r0 - 1:r0, :]
        qp = q[r0:r0 + SB, :] * jnp.exp(b[r0:r0 + SB, :] - bref)
        kp = (kk[0:r0, :] * jnp.exp(bref - b[0:r0, :])).astype(BF16)
        qstk = jnp.concatenate([jnp.where(hmask[h], qp, 0.0) for h in range(HG_HEADS)], axis=0).astype(BF16)
        sc = _dot_nt(qstk, kp)
        oi = _dot(sc.astype(BF16), iv_bf[0:r0, :])
        acc = jnp.where(hmask[0], oi[0:SB, :], 0.0)
        for h in range(1, HG_HEADS):
            acc = acc + jnp.where(hmask[h], oi[h * SB:(h + 1) * SB, :], 0.0)
        parts.append(acc)
    o = o + jnp.concatenate(parts, axis=0)

    st = st_ref[...]
    o = o + _dot_nt((q * jnp.exp(b)).astype(BF16), st.astype(BF16))
    b_end = b[C - 1:C, :]
    kd = (kk * jnp.exp(b_end - b)).astype(BF16)
    upd = _dot_tn(iv_bf, kd)
    st_ref[...] = st * jnp.exp(b_end) + jnp.where(bones > 0, upd, 0.0)

    hi, mid, lo = _split3(o * o)
    ss = _dot(jnp.concatenate([hi, mid, lo], axis=0), bones)
    ms = (ss[0:C, :] + ss[C:2 * C, :] + ss[2 * C:3 * C, :]) * (1.0 / HG_DIM)
    o = (o * lax.rsqrt(ms + EPS)) * ng_ref[...]
    o_ref[...] = (o * (g * _sigmoid(g))).astype(BF16)


def _hgrn2(proj, lb_logits, norm_g, layer):
    bsz, seq, _ = proj.shape
    tril = jnp.tril(jnp.ones((HG_CHUNK, HG_CHUNK), F32))
    head = np.arange(HG_WIDTH) // HG_DIM
    bones = jnp.asarray(head[:, None] == head[None, :], BF16)
    kern = functools.partial(_hgrn2_kernel, layer=layer)
    return pl.pallas_call(
        kern,
        out_shape=jax.ShapeDtypeStruct((bsz, seq, HG_WIDTH), BF16),
        grid=(bsz // MIX_BATCH, seq // HG_CHUNK),
        in_specs=[
            pl.BlockSpec((MIX_BATCH, HG_CHUNK, 4 * HG_WIDTH), lambda b, c: (b, c, P_HG // (4 * HG_WIDTH))),
            pl.BlockSpec((DEPTH, HG_WIDTH), lambda b, c: (0, 0)),
            pl.BlockSpec((1, HG_WIDTH), lambda b, c: (0, 0)),
            pl.BlockSpec((HG_CHUNK, HG_CHUNK), lambda b, c: (0, 0)),
            pl.BlockSpec((HG_WIDTH, HG_WIDTH), lambda b, c: (0, 0)),
        ],
        out_specs=pl.BlockSpec((MIX_BATCH, HG_CHUNK, HG_WIDTH), lambda b, c: (b, c, 0)),
        scratch_shapes=[
            pltpu.VMEM((MIX_BATCH, HG_WIDTH, HG_WIDTH), F32),
            pltpu.VMEM((MIX_BATCH, HG_P_ROWS, HG_WIDTH), BF16),
        ],
        compiler_params=_params(("arbitrary", "arbitrary")),
        name="hgrn2",
    )(proj, lb_logits, norm_g[layer].reshape(1, HG_WIDTH), tril, bones)


_BIT_SWAP_MASK = {16: 0x0000FFFF, 8: 0x00FF00FF, 4: 0x0F0F0F0F, 2: 0x33333333, 1: 0x55555555}


def _dsa_kernel(iq_ref, aq_ref, kv_ref, cos_ref, sin_ref, lng_ref, lnb_ref, o_ref,
                kidx_ref, kr_ref, vt_ref, l2_ref, s2_ref, score_ref, plane_ref, p_ref, thr_ref, cnt_ref, sg_ref,
                *, n_keep, n_blocks):
    j = pl.program_id(1)
    QB = Q_BLOCK
    lane = lax.broadcasted_iota(jnp.int32, (QB, LANES), 1)
    lo64 = lane < 64
    first_half = (lane & 63) < 32
    cos = cos_ref[...]
    sin = sin_ref[...]

    def rope(x):
        rot = jnp.where(first_half, pltpu.roll(x, 96, 1), pltpu.roll(x, 32, 1))
        return x * cos + rot * sin

    def head_pair(x):
        return jnp.concatenate([jnp.where(lo64, x, 0.0), jnp.where(lo64, 0.0, x)], axis=0).astype(BF16)

    @pl.when(j == 0)
    def _():
        kidx_ref[...] = jnp.zeros_like(kidx_ref)
        kr_ref[...] = jnp.zeros_like(kr_ref)
        vt_ref[...] = jnp.zeros_like(vt_ref)

    a = kv_ref[:, 0:LANES]
    mu = jnp.sum(jnp.where(lo64, 0.0, a), axis=-1, keepdims=True) * (1.0 / IDX_DIM)
    xc = a - mu
    var = jnp.sum(jnp.where(lo64, 0.0, xc * xc), axis=-1, keepdims=True) * (1.0 / IDX_DIM)
    y = (xc * lax.rsqrt(var + EPS)) * lng_ref[...] + lnb_ref[...]
    kr = rope(jnp.where(lo64, a, y))
    krs = pltpu.roll(kr, 64, 1)
    kr_ref[j] = jnp.where(lo64, kr, krs).astype(BF16)
    kidx_ref[j] = jnp.where(lo64, krs, kr).astype(BF16)
    bvt = kv_ref[:, LANES:2 * LANES].T
    vt_ref[j] = bvt[0:ATT_DIM, :].astype(BF16)
    w_t = bvt[ATT_DIM:ATT_DIM + IDX_HEADS, :] * (IDX_HEADS ** -0.5 * IDX_DIM ** -0.5)

    def block(nb, j_lo, j_hi):
        nk = nb * QB
        nt = nk // DSA_ROWS
        row_iota = lax.broadcasted_iota(jnp.int32, (DSA_ROWS, QB), 0)
        tpos = j * QB + lax.broadcasted_iota(jnp.int32, (DSA_ROWS, QB), 1)

        def rows(i):
            return pl.ds(pl.multiple_of(i * DSA_ROWS, DSA_ROWS), DSA_ROWS)

        def fold(x, op):
            y = x.reshape(DSA_ROWS // 8, 8, QB)
            acc = y[0]
            for r in range(1, DSA_ROWS // 8):
                acc = op(acc, y[r])
            return acc

        idx_q = [head_pair(rope(iq_ref[:, m * LANES:(m + 1) * LANES])) for m in range(IDX_HEADS // 2)]

        def logits_part(r0, n):
            kpart = kidx_ref[r0 // QB:(r0 + n) // QB].reshape(n, LANES)
            for m in range(IDX_HEADS // 2):
                l2_ref[m, r0:r0 + n, :] = _dot_nt(kpart, idx_q[m])

        def score_tile(r0):
            t = None
            for m in range(IDX_HEADS // 2):
                u = (jnp.maximum(l2_ref[m, r0:r0 + DSA_ROWS, 0:QB], 0.0) * w_t[2 * m:2 * m + 1, :]
                     + jnp.maximum(l2_ref[m, r0:r0 + DSA_ROWS, QB:2 * QB], 0.0) * w_t[2 * m + 1:2 * m + 2, :])
                t = u if t is None else t + u
            t = jnp.where(t == 0.0, 0.0, t)
            score_ref[r0:r0 + DSA_ROWS, :] = jnp.where(r0 + row_iota <= tpos, t, -jnp.inf)

        bounds = list(range(0, nk, DSA_PART)) + [nk]
        for k in range(len(bounds)):
            if k + 1 < len(bounds):
                logits_part(bounds[k], bounds[k + 1] - bounds[k])
            if k > 0:
                for r0 in range(bounds[k - 1], bounds[k], DSA_ROWS):
                    score_tile(r0)

        def count(pred_fn):
            acc = jnp.zeros((COUNT_ROWS, QB), F32)
            for r0 in range(0, nk, COUNT_ROWS):
                acc = acc + jnp.where(pred_fn(score_ref[r0:r0 + COUNT_ROWS, :], r0), 1.0, 0.0)
            return jnp.sum(acc, axis=0, keepdims=True)

        def ordered_bits(v):
            return v ^ ((v >> 31) & jnp.int32(0x7FFFFFFF))

        def key_value(k):
            return lax.bitcast_convert_type(ordered_bits(k), F32)

        def attention_logits():
            krd = kr_ref[0:nb].reshape(nk, LANES)
            for m in range(ATT_HEADS // 2):
                qr = rope(aq_ref[:, m * LANES:(m + 1) * LANES]) * (ATT_DIM ** -0.5)
                s2_ref[m, 0:nk, :] = _dot_nt(krd, head_pair(qr))

        def keep_all():
            thr_ref[...] = jnp.full((1, QB), -jnp.inf, F32)
            sg_ref[...] = jnp.full((1, QB), -1, jnp.int32)

        def search(with_attention_logits):
            n_groups = nk // (32 * 8)

            def plane_group(g, carry):
                base = pl.multiple_of(g * 256, 256)
                x = [ordered_bits(lax.bitcast_convert_type(score_ref[pl.ds(base + 8 * i, 8), :], jnp.int32))
                     ^ jnp.int32(INT_MIN) for i in range(32)]
                step = 16
                while step:
                    msk = jnp.int32(_BIT_SWAP_MASK[step])
                    for k in range(32):
                        if k & step == 0:
                            a, b = x[k], x[k | step]
                            t = (lax.shift_right_logical(a, jnp.int32(step)) ^ b) & msk
                            x[k | step] = b ^ t
                            x[k] = a ^ lax.shift_left(t, jnp.int32(step))
                    step >>= 1
                for b in range(32):
                    plane_ref[b, pl.ds(pl.multiple_of(g * 8, 8), 8), :] = x[b]
                return carry

            lax.fori_loop(0, n_groups, plane_group, 0)

            def popsum(words):
                pc = lax.population_count(words).reshape(n_groups, 8, QB)
                acc = pc[0]
                for g in range(1, n_groups):
                    acc = acc + pc[g]
                return jnp.sum(acc.astype(F32), axis=0, keepdims=True).astype(jnp.int32)

            def body(i, carry):
                active, need, thr = carry
                hit = active & plane_ref[31 - i, 0:8 * n_groups, :]
                ones = popsum(hit)
                take = ones >= need
                active = jnp.where(take, hit, active ^ hit)
                need = jnp.where(take, need, need - ones)
                thr = jnp.where(take, thr | lax.shift_left(jnp.int32(1), 31 - i), thr)
                return active, need, thr

            _, _, thr_key = lax.fori_loop(
                0, 32, body,
                (jnp.full((8 * n_groups, QB), -1, jnp.int32), jnp.full((1, QB), n_keep, jnp.int32),
                 jnp.zeros((1, QB), jnp.int32)))
            thr = key_value(thr_key ^ jnp.int32(INT_MIN))
            if with_attention_logits:
                attention_logits()

            def rank_counts(t):
                return count(lambda s, r0: s > t), count(lambda s, r0: s >= t)

            n_gt, n_ge = rank_counts(thr)
            thr_ref[...] = thr
            cnt_ref[0:1, :] = n_gt
            cnt_ref[1:2, :] = n_ge
            wrong = jnp.max(jnp.where(n_gt < n_keep, jnp.where(n_ge >= n_keep, 0.0, 1.0), 1.0))

            @pl.when(wrong > 0.5)
            def _():
                n_pos = count(lambda s, r0: s >= 0.0)
                lo0 = jnp.where(n_pos >= n_keep, jnp.int32(0), jnp.int32(INT_MIN))

                def bisect(i, lo):
                    cand = lo | lax.shift_left(jnp.int32(1), 30 - i)
                    t = key_value(cand)
                    return jnp.where(count(lambda s, r0: s >= t) >= n_keep, cand, lo)

                t = key_value(lax.fori_loop(0, 31, bisect, lo0))
                n_gt2, n_ge2 = rank_counts(t)
                thr_ref[...] = t
                cnt_ref[0:1, :] = n_gt2
                cnt_ref[1:2, :] = n_ge2

            thr = thr_ref[...]
            need = n_keep - cnt_ref[0:1, :]
            n_eq = cnt_ref[1:2, :] - cnt_ref[0:1, :]
            sg_ref[...] = jnp.full((1, QB), nk, jnp.int32)

            @pl.when(jnp.max(n_eq - need) > 0.5)
            def _():
                crow = lax.broadcasted_iota(jnp.int32, (COUNT_ROWS, QB), 0)

                def body2(i, sg):
                    cand = sg | lax.shift_left(jnp.int32(1), 10 - i)
                    cnt = count(lambda s, r0: jnp.where(r0 + crow < cand, s, -jnp.inf) == thr)
                    return jnp.where(cnt < need, cand, sg)

                sg_ref[...] = lax.fori_loop(0, 11, body2, jnp.zeros((1, QB), jnp.int32))

        if (j_hi - 1) * QB < n_keep:
            keep_all()
            attention_logits()
        elif j_lo * QB >= n_keep:
            search(True)
        else:
            pl.when(j * QB < n_keep)(keep_all)
            pl.when(j * QB >= n_keep)(functools.partial(search, False))
            attention_logits()
        thr = thr_ref[...]
        sg = sg_ref[...]

        def head_cols(h):
            return h // 2, slice((h % 2) * QB, (h % 2 + 1) * QB)

        def mask_tile(i, mx):
            sc = score_ref[rows(i), :]
            tie = jnp.where(i * DSA_ROWS + row_iota <= sg, jnp.inf, NEG_BIG)
            bias = jnp.where(sc > thr, jnp.inf, jnp.where(sc == thr, tie, NEG_BIG))
            out = []
            for h in range(ATT_HEADS):
                m, cols = head_cols(h)
                s = jnp.minimum(s2_ref[m, rows(i), cols], bias)
                s2_ref[m, rows(i), cols] = s
                out.append(jnp.maximum(mx[h], fold(s, jnp.maximum)))
            return tuple(out)

        mx = lax.fori_loop(0, nt, mask_tile, tuple(jnp.full((8, QB), -jnp.inf, F32) for _ in range(ATT_HEADS)))
        mx = [jnp.max(v, axis=0, keepdims=True) for v in mx]

        def prob_tile(i, ls):
            out = []
            for h in range(ATT_HEADS):
                m, cols = head_cols(h)
                p = jnp.exp(s2_ref[m, rows(i), cols] - mx[h])
                p_ref[rows(i), h * QB:(h + 1) * QB] = p.astype(BF16)
                out.append(ls[h] + fold(p, jnp.add))
            return tuple(out)

        ls = lax.fori_loop(0, nt, prob_tile, tuple(jnp.zeros((8, QB), F32) for _ in range(ATT_HEADS)))
        vt = jnp.concatenate([vt_ref[i] for i in range(nb)], axis=1)
        o4 = _dot(vt, p_ref[0:nk, :])
        outs = [o4[:, h * QB:(h + 1) * QB] * (1.0 / jnp.sum(ls[h], axis=0, keepdims=True))
                for h in range(ATT_HEADS)]
        o_ref[...] = jnp.concatenate(outs, axis=0).T.astype(BF16)

    per_class = n_blocks // DSA_CLASSES
    for cls in range(DSA_CLASSES):
        @pl.when((j >= cls * per_class) & (j < (cls + 1) * per_class))
        def _(cls=cls):
            block((cls + 1) * per_class, cls * per_class, (cls + 1) * per_class)


def _rope_tables(seq):
    inv_freq = 1.0 / (ROPE_THETA ** (jnp.arange(0, ATT_DIM, 2, dtype=F32) / ATT_DIM))
    ang = jnp.arange(seq, dtype=F32)[:, None] * inv_freq[None, :]
    cos, sin = jnp.cos(ang), jnp.sin(ang)
    return jnp.concatenate([cos] * 4, axis=1), jnp.concatenate([-sin, sin, -sin, sin], axis=1)


def _dsa(proj, cos, sin, kn_g, kn_b, layer):
    bsz, seq, _ = proj.shape
    n_blocks = seq // Q_BLOCK
    n_keep = min(TOPK_MAX, seq // 4)
    pad = jnp.zeros((IDX_DIM,), F32)
    lng = jnp.concatenate([pad, kn_g[layer]]).reshape(1, LANES)
    lnb = jnp.concatenate([pad, kn_b[layer]]).reshape(1, LANES)
    kern = functools.partial(_dsa_kernel, n_keep=n_keep, n_blocks=n_blocks)
    col = lambda width, off: pl.BlockSpec((None, Q_BLOCK, width), lambda b, j: (b, j, off // width))
    return pl.pallas_call(
        kern,
        out_shape=jax.ShapeDtypeStruct((bsz, seq, ATT_WIDTH), BF16),
        grid=(bsz, n_blocks),
        in_specs=[
            col(IDX_HEADS * IDX_DIM, P_IQ), col(ATT_WIDTH, P_AQ), col(2 * LANES, P_KV),
            pl.BlockSpec((Q_BLOCK, LANES), lambda b, j: (j, 0)),
            pl.BlockSpec((Q_BLOCK, LANES), lambda b, j: (j, 0)),
            pl.BlockSpec((1, LANES), lambda b, j: (0, 0)),
            pl.BlockSpec((1, LANES), lambda b, j: (0, 0)),
        ],
        out_specs=pl.BlockSpec((None, Q_BLOCK, ATT_WIDTH), lambda b, j: (b, j, 0)),
        scratch_shapes=[
            pltpu.VMEM((n_blocks, Q_BLOCK, LANES), BF16),
            pltpu.VMEM((n_blocks, Q_BLOCK, LANES), BF16),
            pltpu.VMEM((n_blocks, ATT_DIM, Q_BLOCK), BF16),
            pltpu.VMEM((IDX_HEADS // 2, seq, 2 * Q_BLOCK), F32),
            pltpu.VMEM((ATT_HEADS // 2, seq, 2 * Q_BLOCK), F32),
            pltpu.VMEM((seq, Q_BLOCK), F32),
            pltpu.VMEM((32, seq // 32, Q_BLOCK), jnp.int32),
            pltpu.VMEM((seq, ATT_HEADS * Q_BLOCK), BF16),
            pltpu.VMEM((1, Q_BLOCK), F32),
            pltpu.VMEM((2, Q_BLOCK), F32),
            pltpu.VMEM((1, Q_BLOCK), jnp.int32),
        ],
        compiler_params=_params(("arbitrary", "arbitrary")),
        name="dsa",
    )(proj, proj, proj, cos, sin, lng, lnb)


def _ssd_kernel(xbc_ref, z_ref, dt_ref, cw_ref, cb_ref, dtb_ref, alog_ref, dsk_ref, ng_ref, tril_ref,
                o_ref, xpad_ref, st_ref):
    @pl.when(pl.program_id(1) == 0)
    def _():
        xpad_ref[...] = jnp.zeros_like(xpad_ref)
        st_ref[...] = jnp.zeros_like(st_ref)

    for s in range(xbc_ref.shape[0]):
        _ssd_chunk(xbc_ref.at[s], z_ref.at[s], dt_ref.at[s], cw_ref, cb_ref, dtb_ref, alog_ref, dsk_ref, ng_ref,
                   tril_ref, o_ref.at[s], xpad_ref.at[s], st_ref.at[s])


def _ssd_chunk(xbc_ref, z_ref, dt_ref, cw_ref, cb_ref, dtb_ref, alog_ref, dsk_ref, ng_ref, tril_ref,
               o_ref, xpad_ref, st_ref):
    L = SSM_CHUNK
    HW = SSM_WIDTH // SSM_GROUPS

    x = xbc_ref[...]
    conv = cb_ref[...] + cw_ref[SSM_CONV - 1:SSM_CONV, :] * x
    for d in range(1, SSM_CONV):
        conv = conv + cw_ref[SSM_CONV - 1 - d:SSM_CONV - d, :] * pltpu.roll(x, d, 0)
    hx = jnp.concatenate([xpad_ref[...], x[0:8, :]], axis=0)
    head = cb_ref[...]
    for i in range(SSM_CONV):
        off = 8 - (SSM_CONV - 1) + i
        head = head + cw_ref[i:i + 1, :] * hx[off:off + 8, :]
    conv = jnp.concatenate([head, conv[8:, :]], axis=0)
    xpad_ref[...] = x[L - 8:L, :]
    xc = conv * _sigmoid(conv)
    xs = xc[:, 0:SSM_WIDTH]

    dtr = dt_ref[...] + dtb_ref[...]
    dt128 = jnp.maximum(dtr, 0.0) + jnp.log(1.0 + jnp.exp(-jnp.abs(dtr)))
    da128 = dt128 * (-jnp.exp(alog_ref[...]))
    acs128 = jnp.dot(tril_ref[...], da128, preferred_element_type=F32, precision=HIGHEST)
    acs_t = acs128.T

    lane = lax.broadcasted_iota(jnp.int32, (L, LANES), 1)
    lo64 = lane < 64

    def col(x, h):
        return jnp.broadcast_to(x[:, h:h + 1], (L, LANES))

    def wide(x):
        return jnp.concatenate([jnp.where(lo64, col(x, 2 * m), col(x, 2 * m + 1)) for m in range(SSM_HEADS // 2)],
                               axis=1)

    dt_w = wide(dt128)
    acs_w = wide(acs128)
    aend_w = acs_w[L - 1:L, :]
    xdt = xs * dt_w
    xdt_dec = (xdt * jnp.exp(aend_w - acs_w)).astype(BF16)
    xdt_bf = xdt.astype(BF16)
    eacs_w = jnp.exp(acs_w)
    eaend_w = jnp.exp(aend_w)

    tri = lax.broadcasted_iota(jnp.int32, (L, L), 1) <= lax.broadcasted_iota(jnp.int32, (L, L), 0)

    def decay(h):
        diff = col(acs128, h) - jnp.broadcast_to(acs_t[h:h + 1, :], (L, L))
        return jnp.where(tri, jnp.exp(jnp.minimum(diff, 0.0)), 0.0)

    ys = []
    for g in range(SSM_GROUPS):
        bm = xc[:, SSM_WIDTH + g * SSM_STATE:SSM_WIDTH + (g + 1) * SSM_STATE]
        cm = xc[:, SSM_WIDTH + (SSM_GROUPS + g) * SSM_STATE:SSM_WIDTH + (SSM_GROUPS + g + 1) * SSM_STATE]
        cm_bf = cm.astype(BF16)
        cb = _dot_nt(cm_bf, bm.astype(BF16))
        st_g = st_ref[:, g * HW:(g + 1) * HW]
        y_off = _dot(cm_bf, st_g.astype(BF16)) * eacs_w[:, g * HW:(g + 1) * HW]
        new_st = _dot(bm.T.astype(BF16), xdt_dec[:, g * HW:(g + 1) * HW])
        st_ref[:, g * HW:(g + 1) * HW] = st_g * eaend_w[:, g * HW:(g + 1) * HW] + new_st
        slabs = []
        for mm in range(HW // LANES):
            m = g * (HW // LANES) + mm
            xslab = xdt_bf[:, m * LANES:(m + 1) * LANES]
            y0 = _dot((cb * decay(2 * m)).astype(BF16), xslab)
            y1 = _dot((cb * decay(2 * m + 1)).astype(BF16), xslab)
            slabs.append(jnp.where(lo64, y0, y1))
        ys.append(jnp.concatenate(slabs, axis=1) + y_off)
    y = jnp.concatenate(ys, axis=1) + xs * dsk_ref[...]
    z = z_ref[...]
    y = y * (z * _sigmoid(z))
    outs = []
    for g in range(SSM_GROUPS):
        yg = y[:, g * HW:(g + 1) * HW]
        yn = yg * lax.rsqrt(jnp.mean(yg * yg, axis=-1, keepdims=True) + EPS)
        outs.append(yn * ng_ref[:, g * HW:(g + 1) * HW])
    o_ref[...] = jnp.concatenate(outs, axis=1).astype(BF16)


def _ssd(proj, conv_w, conv_b, dt_bias, a_log, d_skip, norm_g):
    bsz, seq, _ = proj.shape
    pad = jnp.zeros((LANES - SSM_HEADS,), F32)
    dtb = jnp.concatenate([dt_bias, pad]).reshape(1, LANES)
    alog = jnp.concatenate([a_log, pad]).reshape(1, LANES)
    dsk = jnp.repeat(d_skip, SSM_HEADDIM).reshape(1, SSM_WIDTH)
    tril = jnp.tril(jnp.ones((SSM_CHUNK, SSM_CHUNK), F32))
    col = lambda width, off: pl.BlockSpec((MIX_BATCH, SSM_CHUNK, width), lambda b, c: (b, c, off // width))
    full = lambda r, w: pl.BlockSpec((r, w), lambda b, c: (0, 0))
    return pl.pallas_call(
        _ssd_kernel,
        out_shape=jax.ShapeDtypeStruct((bsz, seq, SSM_WIDTH), BF16),
        grid=(bsz // MIX_BATCH, seq // SSM_CHUNK),
        in_specs=[
            col(CONV_CH, P_XBC), col(SSM_WIDTH, P_Z), col(LANES, P_DT),
            full(SSM_CONV, CONV_CH), full(1, CONV_CH), full(1, LANES), full(1, LANES),
            full(1, SSM_WIDTH), full(1, SSM_WIDTH), full(SSM_CHUNK, SSM_CHUNK),
        ],
        out_specs=pl.BlockSpec((MIX_BATCH, SSM_CHUNK, SSM_WIDTH), lambda b, c: (b, c, 0)),
        scratch_shapes=[
            pltpu.VMEM((MIX_BATCH, 8, CONV_CH), F32),
            pltpu.VMEM((MIX_BATCH, SSM_STATE, SSM_WIDTH), F32),
        ],
        compiler_params=_params(("arbitrary", "arbitrary")),
        name="ssd",
    )(proj, proj, proj, conv_w, conv_b.reshape(1, CONV_CH), dtb, alog, dsk, norm_g.reshape(1, SSM_WIDTH), tril)


def kernel(x, c, w_ada, b_ada, norm_g, w_ffn_gu, w_ffn_down, w_in, w_out, lb_logits, hg_norm_g, idx_k_norm_g,
           idx_k_norm_b, conv_w, conv_b, dt_bias, a_log, d_skip, ssm_norm_g, final_norm_g):
    bsz, seq, _ = x.shape
    mod4 = _ada(c, w_ada, b_ada).reshape(DEPTH, bsz, 9, D_MODEL)
    cos, sin = _rope_tables(seq)
    fg = final_norm_g.reshape(1, D_MODEL)
    for l in range(DEPTH):
        x = _ffn(x, mod4, norm_g, w_ffn_gu[l, 0].astype(BF16), w_ffn_down[l, 0].astype(BF16), fg, l, 0, False)
        proj = _inproj(x, mod4, norm_g, _pack_w_in(w_in[l]), l)
        o_a = _hgrn2(proj, lb_logits, hg_norm_g, l)
        o_b = _dsa(proj, cos, sin, idx_k_norm_g, idx_k_norm_b, l)
        o_c = _ssd(proj, conv_w[l], conv_b[l], dt_bias[l], a_log[l], d_skip[l], ssm_norm_g[l])
        x = _mix_ffn(x, o_a, o_b, o_c, w_out[l].astype(BF16), mod4, norm_g, w_ffn_gu[l, 1].astype(BF16),
                     w_ffn_down[l, 1].astype(BF16), fg, l, l == DEPTH - 1)
    return x
```

```python
import functools

import numpy as np
import jax
import jax.numpy as jnp
from jax import lax
from jax.experimental import pallas as pl
from jax.experimental.pallas import tpu as pltpu

F32 = jnp.float32
BF16 = jnp.bfloat16
HIGHEST = lax.Precision.HIGHEST

D_MODEL = 1024
DEPTH = 2
HG_HEADS = 4
HG_DIM = 64
HG_WIDTH = HG_HEADS * HG_DIM
HG_CHUNK = 64
HG_SUB = 16
assert HG_SUB == 16
HG_P_ROWS = 8 * HG_CHUNK + 8 * (HG_CHUNK // 2)
ATT_HEADS = 4
ATT_DIM = 64
ATT_WIDTH = ATT_HEADS * ATT_DIM
IDX_HEADS = 8
IDX_DIM = 64
TOPK_MAX = 256
Q_BLOCK = 128
ROPE_THETA = 10000.0
SSM_HEADS = 8
SSM_HEADDIM = 64
SSM_WIDTH = SSM_HEADS * SSM_HEADDIM
SSM_GROUPS = 2
SSM_STATE = 128
SSM_CONV = 4
SSM_CHUNK = 128
CONV_CH = SSM_WIDTH + 2 * SSM_GROUPS * SSM_STATE
MIX_WIDTH = HG_WIDTH + ATT_WIDTH + SSM_WIDTH
D_FF = 2816
EPS = 1e-6
NEG_BIG = -1e30
INT_MIN = -(2 ** 31)

LANES = 128
VMEM_LIMIT_BYTES = 56 * 1024 * 1024

P_HG = 0
P_IQ = 1024
P_AQ = 1536
P_KV = 1792
P_XBC = 2048
P_Z = 3072
P_DT = 3584
P_TOTAL = 3712

MIX_BATCH = 4
DSA_CLASSES = 8
DSA_PART = 512
DSA_ROWS = 128
COUNT_ROWS = 64
TOK_TILE = 1024
FFN_TILE = 1024
FFN_CHUNK = 256


def _sigmoid(x):
    return 1.0 / (1.0 + jnp.exp(-x))


def _norm_mod(x, g, shift, scale):
    y = x * lax.rsqrt(jnp.mean(x * x, axis=-1, keepdims=True) + EPS)
    return (y * g) * (1.0 + scale) + shift


def _dot(a, b):
    return jnp.dot(a, b, preferred_element_type=F32)


def _dot_nt(a, b):
    return lax.dot_general(a, b, (((1,), (1,)), ((), ())), preferred_element_type=F32)


def _dot_tn(a, b):
    return lax.dot_general(a, b, (((0,), (0,)), ((), ())), preferred_element_type=F32)


def _params(sem):
    return pltpu.CompilerParams(dimension_semantics=sem, vmem_limit_bytes=VMEM_LIMIT_BYTES)


def _const_spec(shape):
    nd = len(shape)
    return pl.BlockSpec(shape, lambda *_: (0,) * nd, pipeline_mode=pl.Buffered(1))


def _ada_kernel(c_ref, w_ref, b_ref, o_ref):
    c = c_ref[...]
    cond = c * _sigmoid(c)
    o_ref[...] = jnp.dot(cond, w_ref[...], preferred_element_type=F32, precision=HIGHEST) + b_ref[...]


def _ada(c, w_ada, b_ada):
    bsz = c.shape[0]
    n_mod = w_ada.shape[-1]
    nt = n_mod // D_MODEL
    return pl.pallas_call(
        _ada_kernel,
        out_shape=jax.ShapeDtypeStruct((DEPTH, bsz, n_mod), F32),
        grid=(DEPTH, nt),
        in_specs=[
            pl.BlockSpec((bsz, D_MODEL), lambda l, n: (0, 0)),
            pl.BlockSpec((None, D_MODEL, D_MODEL), lambda l, n: (l, 0, n)),
            pl.BlockSpec((None, 1, D_MODEL), lambda l, n: (l, 0, n)),
        ],
        out_specs=pl.BlockSpec((None, bsz, D_MODEL), lambda l, n: (l, 0, n)),
        compiler_params=_params(("arbitrary", "arbitrary")),
        name="ada_mod",
    )(c, w_ada, b_ada.reshape(DEPTH, 1, n_mod))


def _ffn_kernel(x_ref, mod_ref, ng_ref, wgu_ref, wd_ref, fg_ref, o_ref, act_ref, *, sub, final):
    _ffn_body(x_ref[...], mod_ref, ng_ref, wgu_ref, wd_ref, fg_ref, o_ref, act_ref, sub, final)


def _mix_ffn_kernel(x_ref, oa_ref, ob_ref, oc_ref, wo_ref, mod_ref, ng_ref, wgu_ref, wd_ref, fg_ref, o_ref, act_ref,
                    *, final):
    acc = _dot(oa_ref[...], wo_ref[0:HG_WIDTH, :])
    acc += _dot(ob_ref[...], wo_ref[HG_WIDTH:HG_WIDTH + ATT_WIDTH, :])
    acc += _dot(oc_ref[...], wo_ref[HG_WIDTH + ATT_WIDTH:MIX_WIDTH, :])
    x = x_ref[...] + mod_ref[5:6, :] * acc
    _ffn_body(x, mod_ref, ng_ref, wgu_ref, wd_ref, fg_ref, o_ref, act_ref, 2, final)


def _ffn_body(x, mod_ref, ng_ref, wgu_ref, wd_ref, fg_ref, o_ref, act_ref, sub, final):
    h = _norm_mod(x, ng_ref[sub:sub + 1, :], mod_ref[3 * sub:3 * sub + 1, :], mod_ref[3 * sub + 1:3 * sub + 2, :])
    h = h.astype(BF16)
    for c0 in range(0, D_FF, FFN_CHUNK):
        gate = _dot(h, wgu_ref[:, c0:c0 + FFN_CHUNK])
        up = _dot(h, wgu_ref[:, D_FF + c0:D_FF + c0 + FFN_CHUNK])
        act_ref[:, c0:c0 + FFN_CHUNK] = (gate * _sigmoid(gate) * up).astype(BF16)
    out = _dot(act_ref[...], wd_ref[...])
    y = x + (0.5 * mod_ref[3 * sub + 2:3 * sub + 3, :]) * out
    if final:
        y = (y * lax.rsqrt(jnp.mean(y * y, axis=-1, keepdims=True) + EPS)) * fg_ref[...]
    o_ref[...] = y


def _ffn(x, mod4, norm_g, wgu, wd, final_g, layer, sub, final):
    bsz, seq, _ = x.shape
    kern = functools.partial(_ffn_kernel, sub=sub, final=final)
    return pl.pallas_call(
        kern,
        out_shape=jax.ShapeDtypeStruct(x.shape, F32),
        grid=(bsz, seq // FFN_TILE),
        in_specs=[
            pl.BlockSpec((None, FFN_TILE, D_MODEL), lambda b, i: (b, i, 0)),
            pl.BlockSpec((None, None, 9, D_MODEL), lambda b, i: (layer, b, 0, 0)),
            pl.BlockSpec((None, 3, D_MODEL), lambda b, i: (layer, 0, 0)),
            _const_spec((D_MODEL, 2 * D_FF)),
            _const_spec((D_FF, D_MODEL)),
            _const_spec((1, D_MODEL)),
        ],
        out_specs=pl.BlockSpec((None, FFN_TILE, D_MODEL), lambda b, i: (b, i, 0)),
        scratch_shapes=[pltpu.VMEM((FFN_TILE, D_FF), BF16)],
        compiler_params=_params(("arbitrary", "arbitrary")),
        name="ffn",
    )(x, mod4, norm_g, wgu, wd, final_g)


def _mix_ffn(x, o_a, o_b, o_c, w_out, mod4, norm_g, wgu, wd, final_g, layer, final):
    bsz, seq, _ = x.shape
    tile = lambda w: pl.BlockSpec((None, FFN_TILE, w), lambda b, i: (b, i, 0))
    return pl.pallas_call(
        functools.partial(_mix_ffn_kernel, final=final),
        out_shape=jax.ShapeDtypeStruct(x.shape, F32),
        grid=(bsz, seq // FFN_TILE),
        in_specs=[
            tile(D_MODEL), tile(HG_WIDTH), tile(ATT_WIDTH), tile(SSM_WIDTH),
            _const_spec((MIX_WIDTH, D_MODEL)),
            pl.BlockSpec((None, None, 9, D_MODEL), lambda b, i: (layer, b, 0, 0)),
            pl.BlockSpec((None, 3, D_MODEL), lambda b, i: (layer, 0, 0)),
            _const_spec((D_MODEL, 2 * D_FF)),
            _const_spec((D_FF, D_MODEL)),
            _const_spec((1, D_MODEL)),
        ],
        out_specs=tile(D_MODEL),
        scratch_shapes=[pltpu.VMEM((FFN_TILE, D_FF), BF16)],
        compiler_params=_params(("arbitrary", "arbitrary")),
        name="mix_ffn",
    )(x, o_a, o_b, o_c, w_out, mod4, norm_g, wgu, wd, final_g)


def _inproj_kernel(x_ref, mod_ref, ng_ref, w_ref, o_ref):
    h = _norm_mod(x_ref[...], ng_ref[1:2, :], mod_ref[3:4, :], mod_ref[4:5, :]).astype(BF16)
    o_ref[...] = _dot(h, w_ref[...])


def _inproj(x, mod4, norm_g, w_packed, layer):
    bsz, seq, _ = x.shape
    return pl.pallas_call(
        _inproj_kernel,
        out_shape=jax.ShapeDtypeStruct((bsz, seq, P_TOTAL), F32),
        grid=(bsz, seq // TOK_TILE),
        in_specs=[
            pl.BlockSpec((None, TOK_TILE, D_MODEL), lambda b, i: (b, i, 0)),
            pl.BlockSpec((None, None, 9, D_MODEL), lambda b, i: (layer, b, 0, 0)),
            pl.BlockSpec((None, 3, D_MODEL), lambda b, i: (layer, 0, 0)),
            _const_spec((D_MODEL, P_TOTAL)),
        ],
        out_specs=pl.BlockSpec((None, TOK_TILE, P_TOTAL), lambda b, i: (b, i, 0)),
        compiler_params=_params(("arbitrary", "arbitrary")),
        name="in_proj",
    )(x, mod4, norm_g, w_packed)


def _pack_w_in(w):
    z = lambda n: jnp.zeros((w.shape[0], n), w.dtype)
    cols = [
        w[:, 0:1024],
        w[:, 1408:1920],
        w[:, 1024:1280],
        w[:, 1280:1344],
        w[:, 1920:1984],
        w[:, 1344:1408],
        w[:, 1984:1992],
        z(56),
        w[:, 2504:3528],
        w[:, 1992:2504],
        w[:, 3528:3536],
        z(120),
    ]
    return jnp.concatenate(cols, axis=1).astype(BF16)


def _split3(x):
    hi = x.astype(BF16)
    r = x - hi.astype(F32)
    mid = r.astype(BF16)
    lo = (r - mid.astype(F32)).astype(BF16)
    return hi, mid, lo


def _cumsum_rows(tril, x):
    hi, mid, lo = _split3(x)
    n = x.shape[1]
    y = _dot(tril, jnp.concatenate([hi, mid, lo], axis=1))
    return y[:, 0:n] + y[:, n:2 * n] + y[:, 2 * n:3 * n]


def _hgrn2_kernel(blk_ref, lbl_ref, ng_ref, tril_ref, bones_ref, bmask_ref, o_ref, st_ref, p_ref, *, layer):
    @pl.when(pl.program_id(1) == 0)
    def _():
        st_ref[...] = jnp.zeros_like(st_ref)

    lg = lbl_ref[...]
    e = jnp.exp(lg - jnp.max(lg, axis=0, keepdims=True))
    sm = e / jnp.sum(e, axis=0, keepdims=True)
    cs = sm[0:1, :]
    for i in range(1, layer + 1):
        cs = cs + sm[i:i + 1, :]
    lb = cs - sm[0:1, :]

    for s in range(blk_ref.shape[0]):
        _hgrn2_chunk(blk_ref.at[s], lb, ng_ref, tril_ref, bones_ref, bmask_ref, o_ref.at[s], st_ref.at[s],
                     p_ref.at[s])


def _hgrn2_chunk(blk_ref, lb, ng_ref, tril_ref, bones_ref, bmask_ref, o_ref, st_ref, p_ref):
    C, W, SB = HG_CHUNK, HG_WIDTH, HG_SUB
    q = blk_ref[:, 0:W]
    fl = blk_ref[:, W:2 * W]
    iv = blk_ref[:, 2 * W:3 * W]
    g = blk_ref[:, 3 * W:4 * W]
    f = lb + (1.0 - lb) * _sigmoid(fl)
    logf = jnp.log(jnp.maximum(f, 1e-30))
    kk = (1.0 - lb) * _sigmoid(-fl)
    b = _cumsum_rows(tril_ref[...], logf)

    lane = lax.broadcasted_iota(jnp.int32, (1, W), 1)
    hmask = [(lane >= h * HG_DIM) & (lane < (h + 1) * HG_DIM) for h in range(HG_HEADS)]
    bones = bones_ref[...]
    iv_bf = iv.astype(BF16)

    n_sub = C // SB
    H = n_sub * 8
    row8 = lax.broadcasted_iota(jnp.int32, (H, W), 0) & 7

    def halves(x):
        return (jnp.concatenate([x[i * SB:i * SB + 8, :] for i in range(n_sub)], axis=0),
                jnp.concatenate([x[i * SB + 8:(i + 1) * SB, :] for i in range(n_sub)], axis=0))

    def rot(x, d):
        if d == 0:
            return x
        return jnp.concatenate([pltpu.roll(x[8 * i:8 * i + 8, :], d, 0) for i in range(n_sub)], axis=0)

    def earlier(lo, hi, d):
        if d < 8:
            rl = rot(lo, d)
            return rl, jnp.where(row8 >= d, rot(hi, d), rl)
        return None, rot(lo, d - 8)

    q_lo, q_hi = halves(q)
    b_lo, b_hi = halves(b)
    kk_lo, kk_hi = halves(kk)
    iv_lo, iv_hi = halves(iv)

    def p_rows(d):
        if d < 8:
            return (C * d, C * d + H), (C * d + H, C * d + 2 * H)
        return None, (C * 8 + H * (d - 8), C * 8 + H * (d - 7))

    p_ref[0:C, :] = (q * kk).astype(BF16)
    for d in range(1, SB):
        sb_lo, sb_hi = earlier(b_lo, b_hi, d)
        sk_lo, sk_hi = earlier(kk_lo, kk_hi, d)
        r_lo, r_hi = p_rows(d)
        p_hi = q_hi * sk_hi * jnp.exp(b_hi - sb_hi)
        if d > 8:
            p_hi = jnp.where(row8 >= d - 8, p_hi, 0.0)
        p_ref[r_hi[0]:r_hi[1], :] = p_hi.astype(BF16)
        if d < 8:
            p_lo = jnp.where(row8 >= d, q_lo * sk_lo * jnp.exp(b_lo - sb_lo), 0.0)
            p_ref[r_lo[0]:r_lo[1], :] = p_lo.astype(BF16)
    wsum = _dot(p_ref[...], bones)
    o_lo = jnp.zeros((H, W), F32)
    o_hi = jnp.zeros((H, W), F32)
    for d in range(1, SB):
        si_lo, si_hi = earlier(iv_lo, iv_hi, d)
        r_lo, r_hi = p_rows(d)
        o_hi = o_hi + wsum[r_hi[0]:r_hi[1], :] * si_hi
        if d < 8:
            o_lo = o_lo + wsum[r_lo[0]:r_lo[1], :] * si_lo
    pieces = []
    for i in range(n_sub):
        pieces += [o_lo[8 * i:8 * i + 8, :], o_hi[8 * i:8 * i + 8, :]]
    o = wsum[0:C, :] * iv + jnp.concatenate(pieces, axis=0)

    parts = [jnp.zeros((SB, W), F32)]
    for i_sub in range(1, C // SB):
        r0 = i_sub * SB
        bref = b[r0 - 1:r0, :]
        qp = q[r0:r0 + SB, :] * jnp.exp(b[r0:r0 + SB, :] - bref)
        kp = (kk[0:r0, :] * jnp.exp(bref - b[0:r0, :])).astype(BF16)
        qstk = jnp.concatenate([jnp.where(hmask[h], qp, 0.0) for h in range(HG_HEADS)], axis=0).astype(BF16)
        sc = _dot_nt(qstk, kp)
        oi = _dot(sc.astype(BF16), iv_bf[0:r0, :])
        acc = jnp.where(hmask[0], oi[0:SB, :], 0.0)
        for h in range(1, HG_HEADS):
            acc = acc + jnp.where(hmask[h], oi[h * SB:(h + 1) * SB, :], 0.0)
        parts.append(acc)
    o = o + jnp.concatenate(parts, axis=0)

    st = st_ref[...]
    o = o + _dot_nt((q * jnp.exp(b)).astype(BF16), st.astype(BF16))
    b_end = b[C - 1:C, :]
    kd = (kk * jnp.exp(b_end - b)).astype(BF16)
    upd = _dot_tn(iv_bf, kd)
    st_ref[...] = st * jnp.exp(b_end) + upd * bmask_ref[...]

    hi, mid, lo = _split3(o * o)
    ss = _dot(jnp.concatenate([hi, mid, lo], axis=0), bones)
    ms = (ss[0:C, :] + ss[C:2 * C, :] + ss[2 * C:3 * C, :]) * (1.0 / HG_DIM)
    o = (o * lax.rsqrt(ms + EPS)) * ng_ref[...]
    o_ref[...] = (o * (g * _sigmoid(g))).astype(BF16)


def _hgrn2(proj, lb_logits, norm_g, layer):
    bsz, seq, _ = proj.shape
    tril = jnp.tril(jnp.ones((HG_CHUNK, HG_CHUNK), BF16))
    head = np.arange(HG_WIDTH) // HG_DIM
    bones = jnp.asarray(head[:, None] == head[None, :], BF16)
    kern = functools.partial(_hgrn2_kernel, layer=layer)
    return pl.pallas_call(
        kern,
        out_shape=jax.ShapeDtypeStruct((bsz, seq, HG_WIDTH), BF16),
        grid=(bsz // MIX_BATCH, seq // HG_CHUNK),
        in_specs=[
            pl.BlockSpec((MIX_BATCH, HG_CHUNK, 4 * HG_WIDTH), lambda b, c: (b, c, P_HG // (4 * HG_WIDTH))),
            pl.BlockSpec((DEPTH, HG_WIDTH), lambda b, c: (0, 0)),
            pl.BlockSpec((1, HG_WIDTH), lambda b, c: (0, 0)),
            pl.BlockSpec((HG_CHUNK, HG_CHUNK), lambda b, c: (0, 0)),
            pl.BlockSpec((HG_WIDTH, HG_WIDTH), lambda b, c: (0, 0)),
            pl.BlockSpec((HG_WIDTH, HG_WIDTH), lambda b, c: (0, 0)),
        ],
        out_specs=pl.BlockSpec((MIX_BATCH, HG_CHUNK, HG_WIDTH), lambda b, c: (b, c, 0)),
        scratch_shapes=[
            pltpu.VMEM((MIX_BATCH, HG_WIDTH, HG_WIDTH), F32),
            pltpu.VMEM((MIX_BATCH, HG_P_ROWS, HG_WIDTH), BF16),
        ],
        compiler_params=_params(("arbitrary", "arbitrary")),
        name="hgrn2",
    )(proj, lb_logits, norm_g[layer].reshape(1, HG_WIDTH), tril, bones, bones.astype(F32))


_BIT_SWAP_MASK = {16: 0x0000FFFF, 8: 0x00FF00FF, 4: 0x0F0F0F0F, 2: 0x33333333, 1: 0x55555555}


def _dsa_kernel(iq_ref, aq_ref, kv_ref, cos_ref, sin_ref, lng_ref, lnb_ref, o_ref,
                kidx_ref, kr_ref, vt_ref, l2_ref, s2_ref, score_ref, plane_ref, p_ref, thr_ref, cnt_ref, sg_ref, mx_ref,
                *, n_keep, n_blocks):
    j = pl.program_id(1)
    QB = Q_BLOCK
    lane = lax.broadcasted_iota(jnp.int32, (QB, LANES), 1)
    lo64 = lane < 64
    first_half = (lane & 63) < 32
    cos = cos_ref[...]
    sin = sin_ref[...]

    def rope(x):
        rot = jnp.where(first_half, pltpu.roll(x, 96, 1), pltpu.roll(x, 32, 1))
        return x * cos + rot * sin

    def head_pair(x):
        return jnp.concatenate([jnp.where(lo64, x, 0.0), jnp.where(lo64, 0.0, x)], axis=0).astype(BF16)

    @pl.when(j == 0)
    def _():
        kidx_ref[...] = jnp.zeros_like(kidx_ref)
        kr_ref[...] = jnp.zeros_like(kr_ref)
        vt_ref[...] = jnp.zeros_like(vt_ref)

    a = kv_ref[:, 0:LANES]
    mu = jnp.sum(jnp.where(lo64, 0.0, a), axis=-1, keepdims=True) * (1.0 / IDX_DIM)
    xc = a - mu
    var = jnp.sum(jnp.where(lo64, 0.0, xc * xc), axis=-1, keepdims=True) * (1.0 / IDX_DIM)
    y = (xc * lax.rsqrt(var + EPS)) * lng_ref[...] + lnb_ref[...]
    kr = rope(jnp.where(lo64, a, y))
    krs = pltpu.roll(kr, 64, 1)
    kr_ref[j] = jnp.where(lo64, kr, krs).astype(BF16)
    kidx_ref[j] = jnp.where(lo64, krs, kr).astype(BF16)
    bvt = kv_ref[:, LANES:2 * LANES].T
    vt_ref[j] = bvt[0:ATT_DIM, :].astype(BF16)
    w_t = bvt[ATT_DIM:ATT_DIM + IDX_HEADS, :] * (IDX_HEADS ** -0.5 * IDX_DIM ** -0.5)

    def block(nb, j_lo, j_hi):
        nk = nb * QB
        nt = nk // DSA_ROWS
        row_iota = lax.broadcasted_iota(jnp.int32, (DSA_ROWS, QB), 0)
        tpos = j * QB + lax.broadcasted_iota(jnp.int32, (DSA_ROWS, QB), 1)

        def rows(i):
            return pl.ds(pl.multiple_of(i * DSA_ROWS, DSA_ROWS), DSA_ROWS)

        def fold(x, op):
            y = x.reshape(DSA_ROWS // 8, 8, QB)
            acc = y[0]
            for r in range(1, DSA_ROWS // 8):
                acc = op(acc, y[r])
            return acc

        idx_q = [head_pair(rope(iq_ref[:, m * LANES:(m + 1) * LANES])) for m in range(IDX_HEADS // 2)]

        def logits_part(r0, n):
            kpart = kidx_ref[r0 // QB:(r0 + n) // QB].reshape(n, LANES)
            for m in range(IDX_HEADS // 2):
                l2_ref[m, r0:r0 + n, :] = _dot_nt(kpart, idx_q[m])

        def score_tile(r0):
            t = None
            for m in range(IDX_HEADS // 2):
                u = (jnp.maximum(l2_ref[m, r0:r0 + DSA_ROWS, 0:QB], 0.0) * w_t[2 * m:2 * m + 1, :]
                     + jnp.maximum(l2_ref[m, r0:r0 + DSA_ROWS, QB:2 * QB], 0.0) * w_t[2 * m + 1:2 * m + 2, :])
                t = u if t is None else t + u
            t = jnp.where(t == 0.0, 0.0, t)
            score_ref[r0:r0 + DSA_ROWS, :] = jnp.where(r0 + row_iota <= tpos, t, -jnp.inf)

        bounds = list(range(0, nk, DSA_PART)) + [nk]
        for k in range(len(bounds)):
            if k + 1 < len(bounds):
                logits_part(bounds[k], bounds[k + 1] - bounds[k])
            if k > 0:
                for r0 in range(bounds[k - 1], bounds[k], DSA_ROWS):
                    score_tile(r0)

        def count(pred_fn):
            acc = jnp.zeros((COUNT_ROWS, QB), F32)
            for r0 in range(0, nk, COUNT_ROWS):
                acc = acc + jnp.where(pred_fn(score_ref[r0:r0 + COUNT_ROWS, :], r0), 1.0, 0.0)
            return jnp.sum(acc, axis=0, keepdims=True)

        def ordered_bits(v):
            return v ^ ((v >> 31) & jnp.int32(0x7FFFFFFF))

        def key_value(k):
            return lax.bitcast_convert_type(ordered_bits(k), F32)

        def attention_logits():
            krd = kr_ref[0:nb].reshape(nk, LANES)
            for m in range(ATT_HEADS // 2):
                qr = rope(aq_ref[:, m * LANES:(m + 1) * LANES]) * (ATT_DIM ** -0.5)
                s2_ref[m, 0:nk, :] = _dot_nt(krd, head_pair(qr))

        def keep_all():
            thr_ref[...] = jnp.full((1, QB), -jnp.inf, F32)
            sg_ref[...] = jnp.full((1, QB), -1, jnp.int32)

        def search(with_attention_logits):
            n_groups = nk // (32 * 8)

            def plane_group(g, carry):
                base = pl.multiple_of(g * 256, 256)
                x = [ordered_bits(lax.bitcast_convert_type(score_ref[pl.ds(base + 8 * i, 8), :], jnp.int32))
                     ^ jnp.int32(INT_MIN) for i in range(32)]
                step = 16
                while step:
                    msk = jnp.int32(_BIT_SWAP_MASK[step])
                    for k in range(32):
                        if k & step == 0:
                            a, b = x[k], x[k | step]
                            t = (lax.shift_right_logical(a, jnp.int32(step)) ^ b) & msk
                            x[k | step] = b ^ t
                            x[k] = a ^ lax.shift_left(t, jnp.int32(step))
                    step >>= 1
                for b in range(32):
                    plane_ref[b, pl.ds(pl.multiple_of(g * 8, 8), 8), :] = x[b]
                return carry

            lax.fori_loop(0, n_groups, plane_group, 0)

            def popsum(words):
                pc = lax.population_count(words).reshape(n_groups, 8, QB)
                acc = pc[0]
                for g in range(1, n_groups):
                    acc = acc + pc[g]
                return jnp.sum(acc.astype(F32), axis=0, keepdims=True).astype(jnp.int32)

            def body(i, carry):
                active, need, thr = carry
                hit = active & plane_ref[31 - i, 0:8 * n_groups, :]
                ones = popsum(hit)
                take = ones >= need
                active = jnp.where(take, hit, active ^ hit)
                need = jnp.where(take, need, need - ones)
                thr = jnp.where(take, thr | lax.shift_left(jnp.int32(1), 31 - i), thr)
                return active, need, thr

            _, _, thr_key = lax.fori_loop(
                0, 32, body,
                (jnp.full((8 * n_groups, QB), -1, jnp.int32), jnp.full((1, QB), n_keep, jnp.int32),
                 jnp.zeros((1, QB), jnp.int32)))
            thr = key_value(thr_key ^ jnp.int32(INT_MIN))
            if with_attention_logits:
                attention_logits()

            def rank_counts(t):
                return count(lambda s, r0: s > t), count(lambda s, r0: s >= t)

            n_gt, n_ge = rank_counts(thr)
            thr_ref[...] = thr
            cnt_ref[0:1, :] = n_gt
            cnt_ref[1:2, :] = n_ge
            wrong = jnp.max(jnp.where(n_gt < n_keep, jnp.where(n_ge >= n_keep, 0.0, 1.0), 1.0))

            @pl.when(wrong > 0.5)
            def _():
                n_pos = count(lambda s, r0: s >= 0.0)
                lo0 = jnp.where(n_pos >= n_keep, jnp.int32(0), jnp.int32(INT_MIN))

                def bisect(i, lo):
                    cand = lo | lax.shift_left(jnp.int32(1), 30 - i)
                    t = key_value(cand)
                    return jnp.where(count(lambda s, r0: s >= t) >= n_keep, cand, lo)

                t = key_value(lax.fori_loop(0, 31, bisect, lo0))
                n_gt2, n_ge2 = rank_counts(t)
                thr_ref[...] = t
                cnt_ref[0:1, :] = n_gt2
                cnt_ref[1:2, :] = n_ge2

            thr = thr_ref[...]
            need = n_keep - cnt_ref[0:1, :]
            n_eq = cnt_ref[1:2, :] - cnt_ref[0:1, :]
            sg_ref[...] = jnp.full((1, QB), nk, jnp.int32)

            @pl.when(jnp.max(n_eq - need) > 0.5)
            def _():
                crow = lax.broadcasted_iota(jnp.int32, (COUNT_ROWS, QB), 0)

                def body2(i, sg):
                    cand = sg | lax.shift_left(jnp.int32(1), 10 - i)
                    cnt = count(lambda s, r0: jnp.where(r0 + crow < cand, s, -jnp.inf) == thr)
                    return jnp.where(cnt < need, cand, sg)

                sg_ref[...] = lax.fori_loop(0, 11, body2, jnp.zeros((1, QB), jnp.int32))

        if (j_hi - 1) * QB < n_keep:
            keep_all()
            attention_logits()
        elif j_lo * QB >= n_keep:
            search(True)
        else:
            pl.when(j * QB < n_keep)(keep_all)
            pl.when(j * QB >= n_keep)(functools.partial(search, False))
            attention_logits()
        thr = thr_ref[...]
        sg = sg_ref[...]

        def head_cols(h):
            return h // 2, slice((h % 2) * QB, (h % 2 + 1) * QB)

        def mask_pass(plain):
            def mask_tile(i, mx):
                sc = score_ref[rows(i), :]
                if plain:
                    bias = jnp.where(sc >= thr, jnp.inf, NEG_BIG)
                else:
                    tie = jnp.where(i * DSA_ROWS + row_iota <= sg, jnp.inf, NEG_BIG)
                    bias = jnp.where(sc > thr, jnp.inf, jnp.where(sc == thr, tie, NEG_BIG))
                out = []
                for h in range(ATT_HEADS):
                    m, cols = head_cols(h)
                    s = jnp.minimum(s2_ref[m, rows(i), cols], bias)
                    s2_ref[m, rows(i), cols] = s
                    out.append(jnp.maximum(mx[h], fold(s, jnp.maximum)))
                return tuple(out)

            mx = lax.fori_loop(0, nt, mask_tile, tuple(jnp.full((8, QB), -jnp.inf, F32) for _ in range(ATT_HEADS)),
                               unroll=2)
            for h in range(ATT_HEADS):
                mx_ref[8 * h:8 * h + 8, :] = mx[h]

        no_tie_break = jnp.min(sg.astype(F32)) >= nk
        pl.when(no_tie_break)(functools.partial(mask_pass, True))
        pl.when(jnp.logical_not(no_tie_break))(functools.partial(mask_pass, False))
        mx = [jnp.max(mx_ref[8 * h:8 * h + 8, :], axis=0, keepdims=True) for h in range(ATT_HEADS)]

        def prob_tile(i, ls):
            out = []
            for h in range(ATT_HEADS):
                m, cols = head_cols(h)
                p = jnp.exp(s2_ref[m, rows(i), cols] - mx[h])
                p_ref[rows(i), h * QB:(h + 1) * QB] = p.astype(BF16)
                out.append(ls[h] + fold(p, jnp.add))
            return tuple(out)

        ls = lax.fori_loop(0, nt, prob_tile, tuple(jnp.zeros((8, QB), F32) for _ in range(ATT_HEADS)), unroll=2)
        vt = jnp.concatenate([vt_ref[i] for i in range(nb)], axis=1)
        o4 = _dot(vt, p_ref[0:nk, :])
        outs = [o4[:, h * QB:(h + 1) * QB] * (1.0 / jnp.sum(ls[h], axis=0, keepdims=True))
                for h in range(ATT_HEADS)]
        o_ref[...] = jnp.concatenate(outs, axis=0).T.astype(BF16)

    per_class = n_blocks // DSA_CLASSES
    for cls in range(DSA_CLASSES):
        @pl.when((j >= cls * per_class) & (j < (cls + 1) * per_class))
        def _(cls=cls):
            block((cls + 1) * per_class, cls * per_class, (cls + 1) * per_class)


def _rope_tables(seq):
    inv_freq = 1.0 / (ROPE_THETA ** (jnp.arange(0, ATT_DIM, 2, dtype=F32) / ATT_DIM))
    ang = jnp.arange(seq, dtype=F32)[:, None] * inv_freq[None, :]
    cos, sin = jnp.cos(ang), jnp.sin(ang)
    return jnp.concatenate([cos] * 4, axis=1), jnp.concatenate([-sin, sin, -sin, sin], axis=1)


def _dsa(proj, cos, sin, kn_g, kn_b, layer):
    bsz, seq, _ = proj.shape
    n_blocks = seq // Q_BLOCK
    n_keep = min(TOPK_MAX, seq // 4)
    pad = jnp.zeros((IDX_DIM,), F32)
    lng = jnp.concatenate([pad, kn_g[layer]]).reshape(1, LANES)
    lnb = jnp.concatenate([pad, kn_b[layer]]).reshape(1, LANES)
    kern = functools.partial(_dsa_kernel, n_keep=n_keep, n_blocks=n_blocks)
    col = lambda width, off: pl.BlockSpec((None, Q_BLOCK, width), lambda b, j: (b, j, off // width))
    return pl.pallas_call(
        kern,
        out_shape=jax.ShapeDtypeStruct((bsz, seq, ATT_WIDTH), BF16),
        grid=(bsz, n_blocks),
        in_specs=[
            col(IDX_HEADS * IDX_DIM, P_IQ), col(ATT_WIDTH, P_AQ), col(2 * LANES, P_KV),
            pl.BlockSpec((Q_BLOCK, LANES), lambda b, j: (j, 0)),
            pl.BlockSpec((Q_BLOCK, LANES), lambda b, j: (j, 0)),
            pl.BlockSpec((1, LANES), lambda b, j: (0, 0)),
            pl.BlockSpec((1, LANES), lambda b, j: (0, 0)),
        ],
        out_specs=pl.BlockSpec((None, Q_BLOCK, ATT_WIDTH), lambda b, j: (b, j, 0)),
        scratch_shapes=[
            pltpu.VMEM((n_blocks, Q_BLOCK, LANES), BF16),
            pltpu.VMEM((n_blocks, Q_BLOCK, LANES), BF16),
            pltpu.VMEM((n_blocks, ATT_DIM, Q_BLOCK), BF16),
            pltpu.VMEM((IDX_HEADS // 2, seq, 2 * Q_BLOCK), F32),
            pltpu.VMEM((ATT_HEADS // 2, seq, 2 * Q_BLOCK), F32),
            pltpu.VMEM((seq, Q_BLOCK), F32),
            pltpu.VMEM((32, seq // 32, Q_BLOCK), jnp.int32),
            pltpu.VMEM((seq, ATT_HEADS * Q_BLOCK), BF16),
            pltpu.VMEM((1, Q_BLOCK), F32),
            pltpu.VMEM((2, Q_BLOCK), F32),
            pltpu.VMEM((1, Q_BLOCK), jnp.int32),
            pltpu.VMEM((8 * ATT_HEADS, Q_BLOCK), F32),
        ],
        compiler_params=_params(("arbitrary", "arbitrary")),
        name="dsa",
    )(proj, proj, proj, cos, sin, lng, lnb)


def _ssd_kernel(xbc_ref, z_ref, dt_ref, cw_ref, cb_ref, dtb_ref, alog_ref, dsk_ref, ng_ref, tril_ref,
                o_ref, xpad_ref, st_ref):
    @pl.when(pl.program_id(1) == 0)
    def _():
        xpad_ref[...] = jnp.zeros_like(xpad_ref)
        st_ref[...] = jnp.zeros_like(st_ref)

    for s in range(xbc_ref.shape[0]):
        _ssd_chunk(xbc_ref.at[s], z_ref.at[s], dt_ref.at[s], cw_ref, cb_ref, dtb_ref, alog_ref, dsk_ref, ng_ref,
                   tril_ref, o_ref.at[s], xpad_ref.at[s], st_ref.at[s])


def _ssd_chunk(xbc_ref, z_ref, dt_ref, cw_ref, cb_ref, dtb_ref, alog_ref, dsk_ref, ng_ref, tril_ref,
               o_ref, xpad_ref, st_ref):
    L = SSM_CHUNK
    HW = SSM_WIDTH // SSM_GROUPS

    x = xbc_ref[...]
    conv = cb_ref[...] + cw_ref[SSM_CONV - 1:SSM_CONV, :] * x
    for d in range(1, SSM_CONV):
        conv = conv + cw_ref[SSM_CONV - 1 - d:SSM_CONV - d, :] * pltpu.roll(x, d, 0)
    hx = jnp.concatenate([xpad_ref[...], x[0:8, :]], axis=0)
    head = cb_ref[...]
    for i in range(SSM_CONV):
        off = 8 - (SSM_CONV - 1) + i
        head = head + cw_ref[i:i + 1, :] * hx[off:off + 8, :]
    conv = jnp.concatenate([head, conv[8:, :]], axis=0)
    xpad_ref[...] = x[L - 8:L, :]
    xc = conv * _sigmoid(conv)
    xs = xc[:, 0:SSM_WIDTH]

    dtr = dt_ref[...] + dtb_ref[...]
    dt128 = jnp.maximum(dtr, 0.0) + jnp.log(1.0 + jnp.exp(-jnp.abs(dtr)))
    da128 = dt128 * (-jnp.exp(alog_ref[...]))
    acs128 = _cumsum_rows(tril_ref[...], da128)
    acs_t = acs128.T

    lane = lax.broadcasted_iota(jnp.int32, (L, LANES), 1)
    lo64 = lane < 64

    def col(x, h):
        return jnp.broadcast_to(x[:, h:h + 1], (L, LANES))

    def wide(x):
        return jnp.concatenate([jnp.where(lo64, col(x, 2 * m), col(x, 2 * m + 1)) for m in range(SSM_HEADS // 2)],
                               axis=1)

    dt_w = wide(dt128)
    acs_w = wide(acs128)
    aend_w = acs_w[L - 1:L, :]
    xdt = xs * dt_w
    xdt_dec = (xdt * jnp.exp(aend_w - acs_w)).astype(BF16)
    xdt_bf = xdt.astype(BF16)
    eacs_w = jnp.exp(acs_w)
    eaend_w = jnp.exp(aend_w)

    tri = lax.broadcasted_iota(jnp.int32, (L, L), 1) <= lax.broadcasted_iota(jnp.int32, (L, L), 0)

    def decay(h):
        diff = col(acs128, h) - jnp.broadcast_to(acs_t[h:h + 1, :], (L, L))
        return jnp.where(tri, jnp.exp(jnp.minimum(diff, 0.0)), 0.0)

    ys = []
    for g in range(SSM_GROUPS):
        bm = xc[:, SSM_WIDTH + g * SSM_STATE:SSM_WIDTH + (g + 1) * SSM_STATE]
        cm = xc[:, SSM_WIDTH + (SSM_GROUPS + g) * SSM_STATE:SSM_WIDTH + (SSM_GROUPS + g + 1) * SSM_STATE]
        cm_bf = cm.astype(BF16)
        cb = _dot_nt(cm_bf, bm.astype(BF16))
        st_g = st_ref[:, g * HW:(g + 1) * HW]
        y_off = _dot(cm_bf, st_g.astype(BF16)) * eacs_w[:, g * HW:(g + 1) * HW]
        new_st = _dot(bm.T.astype(BF16), xdt_dec[:, g * HW:(g + 1) * HW])
        st_ref[:, g * HW:(g + 1) * HW] = st_g * eaend_w[:, g * HW:(g + 1) * HW] + new_st
        slabs = []
        for mm in range(HW // LANES):
            m = g * (HW // LANES) + mm
            xslab = xdt_bf[:, m * LANES:(m + 1) * LANES]
            y0 = _dot((cb * decay(2 * m)).astype(BF16), xslab)
            y1 = _dot((cb * decay(2 * m + 1)).astype(BF16), xslab)
            slabs.append(jnp.where(lo64, y0, y1))
        ys.append(jnp.concatenate(slabs, axis=1) + y_off)
    y = jnp.concatenate(ys, axis=1) + xs * dsk_ref[...]
    z = z_ref[...]
    y = y * (z * _sigmoid(z))
    outs = []
    for g in range(SSM_GROUPS):
        yg = y[:, g * HW:(g + 1) * HW]
        yn = yg * lax.rsqrt(jnp.mean(yg * yg, axis=-1, keepdims=True) + EPS)
        outs.append(yn * ng_ref[:, g * HW:(g + 1) * HW])
    o_ref[...] = jnp.concatenate(outs, axis=1).astype(BF16)


def _ssd(proj, conv_w, conv_b, dt_bias, a_log, d_skip, norm_g):
    bsz, seq, _ = proj.shape
    pad = jnp.zeros((LANES - SSM_HEADS,), F32)
    dtb = jnp.concatenate([dt_bias, pad]).reshape(1, LANES)
    alog = jnp.concatenate([a_log, pad]).reshape(1, LANES)
    dsk = jnp.repeat(d_skip, SSM_HEADDIM).reshape(1, SSM_WIDTH)
    tril = jnp.tril(jnp.ones((SSM_CHUNK, SSM_CHUNK), BF16))
    col = lambda width, off: pl.BlockSpec((MIX_BATCH, SSM_CHUNK, width), lambda b, c: (b, c, off // width))
    full = lambda r, w: pl.BlockSpec((r, w), lambda b, c: (0, 0))
    return pl.pallas_call(
        _ssd_kernel,
        out_shape=jax.ShapeDtypeStruct((bsz, seq, SSM_WIDTH), BF16),
        grid=(bsz // MIX_BATCH, seq // SSM_CHUNK),
        in_specs=[
            col(CONV_CH, P_XBC), col(SSM_WIDTH, P_Z), col(LANES, P_DT),
            full(SSM_CONV, CONV_CH), full(1, CONV_CH), full(1, LANES), full(1, LANES),
            full(1, SSM_WIDTH), full(1, SSM_WIDTH), full(SSM_CHUNK, SSM_CHUNK),
        ],
        out_specs=pl.BlockSpec((MIX_BATCH, SSM_CHUNK, SSM_WIDTH), lambda b, c: (b, c, 0)),
        scratch_shapes=[
            pltpu.VMEM((MIX_BATCH, 8, CONV_CH), F32),
            pltpu.VMEM((MIX_BATCH, SSM_STATE, SSM_WIDTH), F32),
        ],
        compiler_params=_params(("arbitrary", "arbitrary")),
        name="ssd",
    )(proj, proj, proj, conv_w, conv_b.reshape(1, CONV_CH), dtb, alog, dsk, norm_g.reshape(1, SSM_WIDTH), tril)


def kernel(x, c, w_ada, b_ada, norm_g, w_ffn_gu, w_ffn_down, w_in, w_out, lb_logits, hg_norm_g, idx_k_norm_g,
           idx_k_norm_b, conv_w, conv_b, dt_bias, a_log, d_skip, ssm_norm_g, final_norm_g):
    bsz, seq, _ = x.shape
    mod4 = _ada(c, w_ada, b_ada).reshape(DEPTH, bsz, 9, D_MODEL)
    cos, sin = _rope_tables(seq)
    fg = final_norm_g.reshape(1, D_MODEL)
    for l in range(DEPTH):
        x = _ffn(x, mod4, norm_g, w_ffn_gu[l, 0].astype(BF16), w_ffn_down[l, 0].astype(BF16), fg, l, 0, False)
        proj = _inproj(x, mod4, norm_g, _pack_w_in(w_in[l]), l)
        o_a = _hgrn2(proj, lb_logits, hg_norm_g, l)
        o_b = _dsa(proj, cos, sin, idx_k_norm_g, idx_k_norm_b, l)
        o_c = _ssd(proj, conv_w[l], conv_b[l], dt_bias[l], a_log[l], d_skip[l], ssm_norm_g[l])
        x = _mix_ffn(x, o_a, o_b, o_c, w_out[l].astype(BF16), mod4, norm_g, w_ffn_gu[l, 1].astype(BF16),
                     w_ffn_down[l, 1].astype(BF16), fg, l, l == DEPTH - 1)
    return x
```

```python
import functools

import numpy as np
import jax
import jax.numpy as jnp
from jax import lax
from jax.experimental import pallas as pl
from jax.experimental.pallas import tpu as pltpu

F32 = jnp.float32
BF16 = jnp.bfloat16
HIGHEST = lax.Precision.HIGHEST

D_MODEL = 1024
DEPTH = 2
HG_HEADS = 4
HG_DIM = 64
HG_WIDTH = HG_HEADS * HG_DIM
HG_CHUNK = 64
HG_SUB = 16
assert HG_SUB == 16
HG_P_ROWS = 8 * HG_CHUNK + 8 * (HG_CHUNK // 2)
ATT_HEADS = 4
ATT_DIM = 64
ATT_WIDTH = ATT_HEADS * ATT_DIM
IDX_HEADS = 8
IDX_DIM = 64
TOPK_MAX = 256
Q_BLOCK = 128
ROPE_THETA = 10000.0
SSM_HEADS = 8
SSM_HEADDIM = 64
SSM_WIDTH = SSM_HEADS * SSM_HEADDIM
SSM_GROUPS = 2
SSM_STATE = 128
SSM_CONV = 4
SSM_CHUNK = 128
CONV_CH = SSM_WIDTH + 2 * SSM_GROUPS * SSM_STATE
MIX_WIDTH = HG_WIDTH + ATT_WIDTH + SSM_WIDTH
D_FF = 2816
EPS = 1e-6
NEG_BIG = -1e30
INT_MIN = -(2 ** 31)

LANES = 128
VMEM_LIMIT_BYTES = 56 * 1024 * 1024

P_HG = 0
P_IQ = 1024
P_AQ = 1536
P_KV = 1792
P_XBC = 2048
P_Z = 3072
P_DT = 3584
P_TOTAL = 3712

MIX_BATCH = 8
DSA_CLASSES = 8
DSA_PART = 256
DSA_ROWS = 128
COUNT_ROWS = 64
TOK_TILE = 1024
FFN_TILE = 1024
FFN_CHUNK = 256


def _sigmoid(x):
    return 1.0 / (1.0 + jnp.exp(-x))


def _norm_mod(x, g, shift, scale):
    y = x * lax.rsqrt(jnp.mean(x * x, axis=-1, keepdims=True) + EPS)
    return (y * g) * (1.0 + scale) + shift


def _dot(a, b):
    return jnp.dot(a, b, preferred_element_type=F32)


def _dot_nt(a, b):
    return lax.dot_general(a, b, (((1,), (1,)), ((), ())), preferred_element_type=F32)


def _dot_tn(a, b):
    return lax.dot_general(a, b, (((0,), (0,)), ((), ())), preferred_element_type=F32)


def _params(sem):
    return pltpu.CompilerParams(dimension_semantics=sem, vmem_limit_bytes=VMEM_LIMIT_BYTES)


def _const_spec(shape):
    nd = len(shape)
    return pl.BlockSpec(shape, lambda *_: (0,) * nd, pipeline_mode=pl.Buffered(1))


def _ada_kernel(c_ref, w_ref, b_ref, o_ref):
    c = c_ref[...]
    cond = c * _sigmoid(c)
    o_ref[...] = jnp.dot(cond, w_ref[...], preferred_element_type=F32, precision=HIGHEST) + b_ref[...]


def _ada(c, w_ada, b_ada):
    bsz = c.shape[0]
    n_mod = w_ada.shape[-1]
    nt = n_mod // D_MODEL
    return pl.pallas_call(
        _ada_kernel,
        out_shape=jax.ShapeDtypeStruct((DEPTH, bsz, n_mod), F32),
        grid=(DEPTH, nt),
        in_specs=[
            pl.BlockSpec((bsz, D_MODEL), lambda l, n: (0, 0)),
            pl.BlockSpec((None, D_MODEL, D_MODEL), lambda l, n: (l, 0, n)),
            pl.BlockSpec((None, 1, D_MODEL), lambda l, n: (l, 0, n)),
        ],
        out_specs=pl.BlockSpec((None, bsz, D_MODEL), lambda l, n: (l, 0, n)),
        compiler_params=_params(("arbitrary", "arbitrary")),
        name="ada_mod",
    )(c, w_ada, b_ada.reshape(DEPTH, 1, n_mod))


def _ffn_kernel(x_ref, mod_ref, ng_ref, wgu_ref, wd_ref, fg_ref, o_ref, act_ref, *, sub, final):
    _ffn_body(x_ref[...], mod_ref, ng_ref, wgu_ref, wd_ref, fg_ref, o_ref, act_ref, sub, final)


def _mix_ffn_kernel(x_ref, oa_ref, ob_ref, oc_ref, wo_ref, mod_ref, ng_ref, wgu_ref, wd_ref, fg_ref, o_ref, act_ref,
                    *, final):
    acc = _dot(oa_ref[...], wo_ref[0:HG_WIDTH, :])
    acc += _dot(ob_ref[...], wo_ref[HG_WIDTH:HG_WIDTH + ATT_WIDTH, :])
    acc += _dot(oc_ref[...], wo_ref[HG_WIDTH + ATT_WIDTH:MIX_WIDTH, :])
    x = x_ref[...] + mod_ref[5:6, :] * acc
    _ffn_body(x, mod_ref, ng_ref, wgu_ref, wd_ref, fg_ref, o_ref, act_ref, 2, final)


def _ffn_body(x, mod_ref, ng_ref, wgu_ref, wd_ref, fg_ref, o_ref, act_ref, sub, final):
    h = _norm_mod(x, ng_ref[sub:sub + 1, :], mod_ref[3 * sub:3 * sub + 1, :], mod_ref[3 * sub + 1:3 * sub + 2, :])
    h = h.astype(BF16)
    for c0 in range(0, D_FF, FFN_CHUNK):
        gate = _dot(h, wgu_ref[:, c0:c0 + FFN_CHUNK])
        up = _dot(h, wgu_ref[:, D_FF + c0:D_FF + c0 + FFN_CHUNK])
        act_ref[:, c0:c0 + FFN_CHUNK] = (gate * _sigmoid(gate) * up).astype(BF16)
    out = _dot(act_ref[...], wd_ref[...])
    y = x + (0.5 * mod_ref[3 * sub + 2:3 * sub + 3, :]) * out
    if final:
        y = (y * lax.rsqrt(jnp.mean(y * y, axis=-1, keepdims=True) + EPS)) * fg_ref[...]
    o_ref[...] = y


def _ffn(x, mod4, norm_g, wgu, wd, final_g, layer, sub, final):
    bsz, seq, _ = x.shape
    kern = functools.partial(_ffn_kernel, sub=sub, final=final)
    return pl.pallas_call(
        kern,
        out_shape=jax.ShapeDtypeStruct(x.shape, F32),
        grid=(bsz, seq // FFN_TILE),
        in_specs=[
            pl.BlockSpec((None, FFN_TILE, D_MODEL), lambda b, i: (b, i, 0)),
            pl.BlockSpec((None, None, 9, D_MODEL), lambda b, i: (layer, b, 0, 0)),
            pl.BlockSpec((None, 3, D_MODEL), lambda b, i: (layer, 0, 0)),
            _const_spec((D_MODEL, 2 * D_FF)),
            _const_spec((D_FF, D_MODEL)),
            _const_spec((1, D_MODEL)),
        ],
        out_specs=pl.BlockSpec((None, FFN_TILE, D_MODEL), lambda b, i: (b, i, 0)),
        scratch_shapes=[pltpu.VMEM((FFN_TILE, D_FF), BF16)],
        compiler_params=_params(("arbitrary", "arbitrary")),
        name="ffn",
    )(x, mod4, norm_g, wgu, wd, final_g)


def _mix_ffn(x, o_a, o_b, o_c, w_out, mod4, norm_g, wgu, wd, final_g, layer, final):
    bsz, seq, _ = x.shape
    tile = lambda w: pl.BlockSpec((None, FFN_TILE, w), lambda b, i: (b, i, 0))
    return pl.pallas_call(
        functools.partial(_mix_ffn_kernel, final=final),
        out_shape=jax.ShapeDtypeStruct(x.shape, F32),
        grid=(bsz, seq // FFN_TILE),
        in_specs=[
            tile(D_MODEL), tile(HG_WIDTH), tile(ATT_WIDTH), tile(SSM_WIDTH),
            _const_spec((MIX_WIDTH, D_MODEL)),
            pl.BlockSpec((None, None, 9, D_MODEL), lambda b, i: (layer, b, 0, 0)),
            pl.BlockSpec((None, 3, D_MODEL), lambda b, i: (layer, 0, 0)),
            _const_spec((D_MODEL, 2 * D_FF)),
            _const_spec((D_FF, D_MODEL)),
            _const_spec((1, D_MODEL)),
        ],
        out_specs=tile(D_MODEL),
        scratch_shapes=[pltpu.VMEM((FFN_TILE, D_FF), BF16)],
        compiler_params=_params(("arbitrary", "arbitrary")),
        name="mix_ffn",
    )(x, o_a, o_b, o_c, w_out, mod4, norm_g, wgu, wd, final_g)


def _inproj_kernel(x_ref, mod_ref, ng_ref, w_ref, o_ref):
    h = _norm_mod(x_ref[...], ng_ref[1:2, :], mod_ref[3:4, :], mod_ref[4:5, :]).astype(BF16)
    o_ref[...] = _dot(h, w_ref[...])


def _inproj(x, mod4, norm_g, w_packed, layer):
    bsz, seq, _ = x.shape
    return pl.pallas_call(
        _inproj_kernel,
        out_shape=jax.ShapeDtypeStruct((bsz, seq, P_TOTAL), F32),
        grid=(bsz, seq // TOK_TILE),
        in_specs=[
            pl.BlockSpec((None, TOK_TILE, D_MODEL), lambda b, i: (b, i, 0)),
            pl.BlockSpec((None, None, 9, D_MODEL), lambda b, i: (layer, b, 0, 0)),
            pl.BlockSpec((None, 3, D_MODEL), lambda b, i: (layer, 0, 0)),
            _const_spec((D_MODEL, P_TOTAL)),
        ],
        out_specs=pl.BlockSpec((None, TOK_TILE, P_TOTAL), lambda b, i: (b, i, 0)),
        compiler_params=_params(("arbitrary", "arbitrary")),
        name="in_proj",
    )(x, mod4, norm_g, w_packed)


def _pack_w_in(w):
    z = lambda n: jnp.zeros((w.shape[0], n), w.dtype)
    cols = [
        w[:, 0:1024],
        w[:, 1408:1920],
        w[:, 1024:1280],
        w[:, 1280:1344],
        w[:, 1920:1984],
        w[:, 1344:1408],
        w[:, 1984:1992],
        z(56),
        w[:, 2504:3528],
        w[:, 1992:2504],
        w[:, 3528:3536],
        z(120),
    ]
    return jnp.concatenate(cols, axis=1).astype(BF16)


def _split3(x):
    hi = x.astype(BF16)
    r = x - hi.astype(F32)
    mid = r.astype(BF16)
    lo = (r - mid.astype(F32)).astype(BF16)
    return hi, mid, lo


def _cumsum_rows(tril, x):
    hi, mid, lo = _split3(x)
    n = x.shape[1]
    y = _dot(tril, jnp.concatenate([hi, mid, lo], axis=1))
    return y[:, 0:n] + y[:, n:2 * n] + y[:, 2 * n:3 * n]


def _hgrn2_kernel(blk_ref, lbl_ref, ng_ref, tril_ref, bones_ref, bmask_ref, o_ref, st_ref, p_ref, *, layer):
    @pl.when(pl.program_id(1) == 0)
    def _():
        st_ref[...] = jnp.zeros_like(st_ref)

    lg = lbl_ref[...]
    e = jnp.exp(lg - jnp.max(lg, axis=0, keepdims=True))
    sm = e / jnp.sum(e, axis=0, keepdims=True)
    cs = sm[0:1, :]
    for i in range(1, layer + 1):
        cs = cs + sm[i:i + 1, :]
    lb = cs - sm[0:1, :]

    for s in range(blk_ref.shape[0]):
        _hgrn2_chunk(blk_ref.at[s], lb, ng_ref, tril_ref, bones_ref, bmask_ref, o_ref.at[s], st_ref.at[s],
                     p_ref.at[s])


def _hgrn2_chunk(blk_ref, lb, ng_ref, tril_ref, bones_ref, bmask_ref, o_ref, st_ref, p_ref):
    C, W, SB = HG_CHUNK, HG_WIDTH, HG_SUB
    q = blk_ref[:, 0:W]
    fl = blk_ref[:, W:2 * W]
    iv = blk_ref[:, 2 * W:3 * W]
    g = blk_ref[:, 3 * W:4 * W]
    f = lb + (1.0 - lb) * _sigmoid(fl)
    logf = jnp.log(jnp.maximum(f, 1e-30))
    kk = (1.0 - lb) * _sigmoid(-fl)
    b = jnp.dot(tril_ref[...], logf, preferred_element_type=F32, precision=HIGHEST)

    lane = lax.broadcasted_iota(jnp.int32, (1, W), 1)
    hmask = [(lane >= h * HG_DIM) & (lane < (h + 1) * HG_DIM) for h in range(HG_HEADS)]
    bones = bones_ref[...]
    iv_bf = iv.astype(BF16)

    n_sub = C // SB
    H = n_sub * 8
    row8 = lax.broadcasted_iota(jnp.int32, (H, W), 0) & 7

    def halves(x):
        return (jnp.concatenate([x[i * SB:i * SB + 8, :] for i in range(n_sub)], axis=0),
                jnp.concatenate([x[i * SB + 8:(i + 1) * SB, :] for i in range(n_sub)], axis=0))

    def rot(x, d):
        if d == 0:
            return x
        return jnp.concatenate([pltpu.roll(x[8 * i:8 * i + 8, :], d, 0) for i in range(n_sub)], axis=0)

    def earlier(lo, hi, d):
        if d < 8:
            rl = rot(lo, d)
            return rl, jnp.where(row8 >= d, rot(hi, d), rl)
        return None, rot(lo, d - 8)

    q_lo, q_hi = halves(q)
    b_lo, b_hi = halves(b)
    kk_lo, kk_hi = halves(kk)
    iv_lo, iv_hi = halves(iv)

    def p_rows(d):
        if d < 8:
            return (C * d, C * d + H), (C * d + H, C * d + 2 * H)
        return None, (C * 8 + H * (d - 8), C * 8 + H * (d - 7))

    p_ref[0:C, :] = (q * kk).astype(BF16)
    for d in range(1, SB):
        sb_lo, sb_hi = earlier(b_lo, b_hi, d)
        sk_lo, sk_hi = earlier(kk_lo, kk_hi, d)
        r_lo, r_hi = p_rows(d)
        p_hi = q_hi * sk_hi * jnp.exp(b_hi - sb_hi)
        if d > 8:
            p_hi = jnp.where(row8 >= d - 8, p_hi, 0.0)
        p_ref[r_hi[0]:r_hi[1], :] = p_hi.astype(BF16)
        if d < 8:
            p_lo = jnp.where(row8 >= d, q_lo * sk_lo * jnp.exp(b_lo - sb_lo), 0.0)
            p_ref[r_lo[0]:r_lo[1], :] = p_lo.astype(BF16)
    wsum = _dot(p_ref[...], bones)
    o_lo = jnp.zeros((H, W), F32)
    o_hi = jnp.zeros((H, W), F32)
    for d in range(1, SB):
        si_lo, si_hi = earlier(iv_lo, iv_hi, d)
        r_lo, r_hi = p_rows(d)
        o_hi = o_hi + wsum[r_hi[0]:r_hi[1], :] * si_hi
        if d < 8:
            o_lo = o_lo + wsum[r_lo[0]:r_lo[1], :] * si_lo
    pieces = []
    for i in range(n_sub):
        pieces += [o_lo[8 * i:8 * i + 8, :], o_hi[8 * i:8 * i + 8, :]]
    o = wsum[0:C, :] * iv + jnp.concatenate(pieces, axis=0)

    parts = [jnp.zeros((SB, W), F32)]
    for i_sub in range(1, C // SB):
        r0 = i_sub * SB
        bref = b[r0 - 1:r0, :]
        qp = q[r0:r0 + SB, :] * jnp.exp(b[r0:r0 + SB, :] - bref)
        kp = (kk[0:r0, :] * jnp.exp(bref - b[0:r0, :])).astype(BF16)
        qstk = jnp.concatenate([jnp.where(hmask[h], qp, 0.0) for h in range(HG_HEADS)], axis=0).astype(BF16)
        sc = _dot_nt(qstk, kp)
        oi = _dot(sc.astype(BF16), iv_bf[0:r0, :])
        acc = jnp.where(hmask[0], oi[0:SB, :], 0.0)
        for h in range(1, HG_HEADS):
            acc = acc + jnp.where(hmask[h], oi[h * SB:(h + 1) * SB, :], 0.0)
        parts.append(acc)
    o = o + jnp.concatenate(parts, axis=0)

    st = st_ref[...]
    o = o + _dot_nt((q * jnp.exp(b)).astype(BF16), st.astype(BF16))
    b_end = b[C - 1:C, :]
    kd = (kk * jnp.exp(b_end - b)).astype(BF16)
    upd = _dot_tn(iv_bf, kd)
    st_ref[...] = st * jnp.exp(b_end) + upd * bmask_ref[...]

    hi, mid, lo = _split3(o * o)
    ss = _dot(jnp.concatenate([hi, mid, lo], axis=0), bones)
    ms = (ss[0:C, :] + ss[C:2 * C, :] + ss[2 * C:3 * C, :]) * (1.0 / HG_DIM)
    o = (o * lax.rsqrt(ms + EPS)) * ng_ref[...]
    o_ref[...] = (o * (g * _sigmoid(g))).astype(BF16)


def _hgrn2(proj, lb_logits, norm_g, layer):
    bsz, seq, _ = proj.shape
    tril = jnp.tril(jnp.ones((HG_CHUNK, HG_CHUNK), F32))
    head = np.arange(HG_WIDTH) // HG_DIM
    bones = jnp.asarray(head[:, None] == head[None, :], BF16)
    kern = functools.partial(_hgrn2_kernel, layer=layer)
    return pl.pallas_call(
        kern,
        out_shape=jax.ShapeDtypeStruct((bsz, seq, HG_WIDTH), BF16),
        grid=(bsz // MIX_BATCH, seq // HG_CHUNK),
        in_specs=[
            pl.BlockSpec((MIX_BATCH, HG_CHUNK, 4 * HG_WIDTH), lambda b, c: (b, c, P_HG // (4 * HG_WIDTH))),
            pl.BlockSpec((DEPTH, HG_WIDTH), lambda b, c: (0, 0)),
            pl.BlockSpec((1, HG_WIDTH), lambda b, c: (0, 0)),
            pl.BlockSpec((HG_CHUNK, HG_CHUNK), lambda b, c: (0, 0)),
            pl.BlockSpec((HG_WIDTH, HG_WIDTH), lambda b, c: (0, 0)),
            pl.BlockSpec((HG_WIDTH, HG_WIDTH), lambda b, c: (0, 0)),
        ],
        out_specs=pl.BlockSpec((MIX_BATCH, HG_CHUNK, HG_WIDTH), lambda b, c: (b, c, 0)),
        scratch_shapes=[
            pltpu.VMEM((MIX_BATCH, HG_WIDTH, HG_WIDTH), F32),
            pltpu.VMEM((MIX_BATCH, HG_P_ROWS, HG_WIDTH), BF16),
        ],
        compiler_params=_params(("arbitrary", "arbitrary")),
        name="hgrn2",
    )(proj, lb_logits, norm_g[layer].reshape(1, HG_WIDTH), tril, bones, bones.astype(F32))


_BIT_SWAP_MASK = {16: 0x0000FFFF, 8: 0x00FF00FF, 4: 0x0F0F0F0F, 2: 0x33333333, 1: 0x55555555}


def _dsa_kernel(iq_ref, aq_ref, kv_ref, cos_ref, sin_ref, lng_ref, lnb_ref, o_ref,
                kidx_ref, kr_ref, vt_ref, l2_ref, s2_ref, score_ref, plane_ref, p_ref, thr_ref, cnt_ref, sg_ref, mx_ref,
                *, n_keep, n_blocks):
    j = pl.program_id(1)
    QB = Q_BLOCK
    lane = lax.broadcasted_iota(jnp.int32, (QB, LANES), 1)
    lo64 = lane < 64
    first_half = (lane & 63) < 32
    cos = cos_ref[...]
    sin = sin_ref[...]

    def rope(x):
        rot = jnp.where(first_half, pltpu.roll(x, 96, 1), pltpu.roll(x, 32, 1))
        return x * cos + rot * sin

    def head_pair(x):
        return jnp.concatenate([jnp.where(lo64, x, 0.0), jnp.where(lo64, 0.0, x)], axis=0).astype(BF16)

    @pl.when(j == 0)
    def _():
        kidx_ref[...] = jnp.zeros_like(kidx_ref)
        kr_ref[...] = jnp.zeros_like(kr_ref)
        vt_ref[...] = jnp.zeros_like(vt_ref)

    a = kv_ref[:, 0:LANES]
    mu = jnp.sum(jnp.where(lo64, 0.0, a), axis=-1, keepdims=True) * (1.0 / IDX_DIM)
    xc = a - mu
    var = jnp.sum(jnp.where(lo64, 0.0, xc * xc), axis=-1, keepdims=True) * (1.0 / IDX_DIM)
    y = (xc * lax.rsqrt(var + EPS)) * lng_ref[...] + lnb_ref[...]
    kr = rope(jnp.where(lo64, a, y))
    krs = pltpu.roll(kr, 64, 1)
    kr_ref[j] = jnp.where(lo64, kr, krs).astype(BF16)
    kidx_ref[j] = jnp.where(lo64, krs, kr).astype(BF16)
    bvt = kv_ref[:, LANES:2 * LANES].T
    vt_ref[j] = bvt[0:ATT_DIM, :].astype(BF16)
    w_t = bvt[ATT_DIM:ATT_DIM + IDX_HEADS, :] * (IDX_HEADS ** -0.5 * IDX_DIM ** -0.5)

    def block(nb, j_lo, j_hi):
        nk = nb * QB
        nt = nk // DSA_ROWS
        row_iota = lax.broadcasted_iota(jnp.int32, (DSA_ROWS, QB), 0)
        tpos = j * QB + lax.broadcasted_iota(jnp.int32, (DSA_ROWS, QB), 1)

        def rows(i):
            return pl.ds(pl.multiple_of(i * DSA_ROWS, DSA_ROWS), DSA_ROWS)

        def fold(x, op):
            y = x.reshape(DSA_ROWS // 8, 8, QB)
            acc = y[0]
            for r in range(1, DSA_ROWS // 8):
                acc = op(acc, y[r])
            return acc

        idx_q = [head_pair(rope(iq_ref[:, m * LANES:(m + 1) * LANES])) for m in range(IDX_HEADS // 2)]

        def logits_part(r0, n):
            kpart = kidx_ref[r0 // QB:(r0 + n) // QB].reshape(n, LANES)
            for m in range(IDX_HEADS // 2):
                l2_ref[m, r0:r0 + n, :] = _dot_nt(kpart, idx_q[m])

        def score_tile(r0):
            t = None
            for m in range(IDX_HEADS // 2):
                u = (jnp.maximum(l2_ref[m, r0:r0 + DSA_ROWS, 0:QB], 0.0) * w_t[2 * m:2 * m + 1, :]
                     + jnp.maximum(l2_ref[m, r0:r0 + DSA_ROWS, QB:2 * QB], 0.0) * w_t[2 * m + 1:2 * m + 2, :])
                t = u if t is None else t + u
            t = jnp.where(t == 0.0, 0.0, t)
            score_ref[r0:r0 + DSA_ROWS, :] = jnp.where(r0 + row_iota <= tpos, t, -jnp.inf)

        bounds = list(range(0, nk, DSA_PART)) + [nk]
        for k in range(len(bounds)):
            if k + 1 < len(bounds):
                logits_part(bounds[k], bounds[k + 1] - bounds[k])
            if k > 0:
                for r0 in range(bounds[k - 1], bounds[k], DSA_ROWS):
                    score_tile(r0)

        def count(pred_fn):
            acc = jnp.zeros((COUNT_ROWS, QB), F32)
            for r0 in range(0, nk, COUNT_ROWS):
                acc = acc + jnp.where(pred_fn(score_ref[r0:r0 + COUNT_ROWS, :], r0), 1.0, 0.0)
            return jnp.sum(acc, axis=0, keepdims=True)

        def ordered_bits(v):
            return v ^ ((v >> 31) & jnp.int32(0x7FFFFFFF))

        def key_value(k):
            return lax.bitcast_convert_type(ordered_bits(k), F32)

        def attention_logits():
            krd = kr_ref[0:nb].reshape(nk, LANES)
            for m in range(ATT_HEADS // 2):
                qr = rope(aq_ref[:, m * LANES:(m + 1) * LANES]) * (ATT_DIM ** -0.5)
                s2_ref[m, 0:nk, :] = _dot_nt(krd, head_pair(qr))

        def keep_all():
            thr_ref[...] = jnp.full((1, QB), -jnp.inf, F32)
            sg_ref[...] = jnp.full((1, QB), -1, jnp.int32)

        def search(with_attention_logits):
            n_groups = nk // (32 * 8)

            def plane_group(g, carry):
                base = pl.multiple_of(g * 256, 256)
                x = [ordered_bits(lax.bitcast_convert_type(score_ref[pl.ds(base + 8 * i, 8), :], jnp.int32))
                     ^ jnp.int32(INT_MIN) for i in range(32)]
                step = 16
                while step:
                    msk = jnp.int32(_BIT_SWAP_MASK[step])
                    for k in range(32):
                        if k & step == 0:
                            a, b = x[k], x[k | step]
                            t = (lax.shift_right_logical(a, jnp.int32(step)) ^ b) & msk
                            x[k | step] = b ^ t
                            x[k] = a ^ lax.shift_left(t, jnp.int32(step))
                    step >>= 1
                for b in range(32):
                    plane_ref[b, pl.ds(pl.multiple_of(g * 8, 8), 8), :] = x[b]
                return carry

            lax.fori_loop(0, n_groups, plane_group, 0)

            def popsum(words):
                pc = lax.population_count(words).reshape(n_groups, 8, QB)
                acc = pc[0]
                for g in range(1, n_groups):
                    acc = acc + pc[g]
                return jnp.sum(acc.astype(F32), axis=0, keepdims=True).astype(jnp.int32)

            def body(i, carry):
                active, need, thr = carry
                hit = active & plane_ref[31 - i, 0:8 * n_groups, :]
                ones = popsum(hit)
                take = ones >= need
                active = jnp.where(take, hit, active ^ hit)
                need = jnp.where(take, need, need - ones)
                thr = jnp.where(take, thr | lax.shift_left(jnp.int32(1), 31 - i), thr)
                return active, need, thr

            _, _, thr_key = lax.fori_loop(
                0, 32, body,
                (jnp.full((8 * n_groups, QB), -1, jnp.int32), jnp.full((1, QB), n_keep, jnp.int32),
                 jnp.zeros((1, QB), jnp.int32)))
            thr = key_value(thr_key ^ jnp.int32(INT_MIN))
            if with_attention_logits:
                attention_logits()

            def rank_counts(t):
                return count(lambda s, r0: s > t), count(lambda s, r0: s >= t)

            n_gt, n_ge = rank_counts(thr)
            thr_ref[...] = thr
            cnt_ref[0:1, :] = n_gt
            cnt_ref[1:2, :] = n_ge
            wrong = jnp.max(jnp.where(n_gt < n_keep, jnp.where(n_ge >= n_keep, 0.0, 1.0), 1.0))

            @pl.when(wrong > 0.5)
            def _():
                n_pos = count(lambda s, r0: s >= 0.0)
                lo0 = jnp.where(n_pos >= n_keep, jnp.int32(0), jnp.int32(INT_MIN))

                def bisect(i, lo):
                    cand = lo | lax.shift_left(jnp.int32(1), 30 - i)
                    t = key_value(cand)
                    return jnp.where(count(lambda s, r0: s >= t) >= n_keep, cand, lo)

                t = key_value(lax.fori_loop(0, 31, bisect, lo0))
                n_gt2, n_ge2 = rank_counts(t)
                thr_ref[...] = t
                cnt_ref[0:1, :] = n_gt2
                cnt_ref[1:2, :] = n_ge2

            thr = thr_ref[...]
            need = n_keep - cnt_ref[0:1, :]
            n_eq = cnt_ref[1:2, :] - cnt_ref[0:1, :]
            sg_ref[...] = jnp.full((1, QB), nk, jnp.int32)

            @pl.when(jnp.max(n_eq - need) > 0.5)
            def _():
                crow = lax.broadcasted_iota(jnp.int32, (COUNT_ROWS, QB), 0)

                def body2(i, sg):
                    cand = sg | lax.shift_left(jnp.int32(1), 10 - i)
                    cnt = count(lambda s, r0: jnp.where(r0 + crow < cand, s, -jnp.inf) == thr)
                    return jnp.where(cnt < need, cand, sg)

                sg_ref[...] = lax.fori_loop(0, 11, body2, jnp.zeros((1, QB), jnp.int32))

        if (j_hi - 1) * QB < n_keep:
            keep_all()
            attention_logits()
        elif j_lo * QB >= n_keep:
            search(True)
        else:
            pl.when(j * QB < n_keep)(keep_all)
            pl.when(j * QB >= n_keep)(functools.partial(search, False))
            attention_logits()
        thr = thr_ref[...]
        sg = sg_ref[...]

        def head_cols(h):
            return h // 2, slice((h % 2) * QB, (h % 2 + 1) * QB)

        def mask_pass(plain):
            def mask_tile(i, mx):
                sc = score_ref[rows(i), :]
                if plain:
                    bias = jnp.where(sc >= thr, jnp.inf, NEG_BIG)
                else:
                    tie = jnp.where(i * DSA_ROWS + row_iota <= sg, jnp.inf, NEG_BIG)
                    bias = jnp.where(sc > thr, jnp.inf, jnp.where(sc == thr, tie, NEG_BIG))
                out = []
                for h in range(ATT_HEADS):
                    m, cols = head_cols(h)
                    s = jnp.minimum(s2_ref[m, rows(i), cols], bias)
                    s2_ref[m, rows(i), cols] = s
                    out.append(jnp.maximum(mx[h], fold(s, jnp.maximum)))
                return tuple(out)

            mx = lax.fori_loop(0, nt, mask_tile, tuple(jnp.full((8, QB), -jnp.inf, F32) for _ in range(ATT_HEADS)),
                               unroll=2)
            for h in range(ATT_HEADS):
                mx_ref[8 * h:8 * h + 8, :] = mx[h]

        no_tie_break = jnp.min(sg.astype(F32)) >= nk
        pl.when(no_tie_break)(functools.partial(mask_pass, True))
        pl.when(jnp.logical_not(no_tie_break))(functools.partial(mask_pass, False))
        mx = [jnp.max(mx_ref[8 * h:8 * h + 8, :], axis=0, keepdims=True) for h in range(ATT_HEADS)]

        def prob_tile(i, ls):
            out = []
            for h in range(ATT_HEADS):
                m, cols = head_cols(h)
                p = jnp.exp(s2_ref[m, rows(i), cols] - mx[h])
                p_ref[rows(i), h * QB:(h + 1) * QB] = p.astype(BF16)
                out.append(ls[h] + fold(p, jnp.add))
            return tuple(out)

        ls = lax.fori_loop(0, nt, prob_tile, tuple(jnp.zeros((8, QB), F32) for _ in range(ATT_HEADS)), unroll=2)
        vt = jnp.concatenate([vt_ref[i] for i in range(nb)], axis=1)
        o4 = _dot(vt, p_ref[0:nk, :])
        outs = [o4[:, h * QB:(h + 1) * QB] * (1.0 / jnp.sum(ls[h], axis=0, keepdims=True))
                for h in range(ATT_HEADS)]
        o_ref[...] = jnp.concatenate(outs, axis=0).T.astype(BF16)

    per_class = n_blocks // DSA_CLASSES
    for cls in range(DSA_CLASSES):
        @pl.when((j >= cls * per_class) & (j < (cls + 1) * per_class))
        def _(cls=cls):
            block((cls + 1) * per_class, cls * per_class, (cls + 1) * per_class)


def _rope_tables(seq):
    inv_freq = 1.0 / (ROPE_THETA ** (jnp.arange(0, ATT_DIM, 2, dtype=F32) / ATT_DIM))
    ang = jnp.arange(seq, dtype=F32)[:, None] * inv_freq[None, :]
    cos, sin = jnp.cos(ang), jnp.sin(ang)
    return jnp.concatenate([cos] * 4, axis=1), jnp.concatenate([-sin, sin, -sin, sin], axis=1)


def _dsa(proj, cos, sin, kn_g, kn_b, layer):
    bsz, seq, _ = proj.shape
    n_blocks = seq // Q_BLOCK
    n_keep = min(TOPK_MAX, seq // 4)
    pad = jnp.zeros((IDX_DIM,), F32)
    lng = jnp.concatenate([pad, kn_g[layer]]).reshape(1, LANES)
    lnb = jnp.concatenate([pad, kn_b[layer]]).reshape(1, LANES)
    kern = functools.partial(_dsa_kernel, n_keep=n_keep, n_blocks=n_blocks)
    col = lambda width, off: pl.BlockSpec((None, Q_BLOCK, width), lambda b, j: (b, j, off // width))
    return pl.pallas_call(
        kern,
        out_shape=jax.ShapeDtypeStruct((bsz, seq, ATT_WIDTH), BF16),
        grid=(bsz, n_blocks),
        in_specs=[
            col(IDX_HEADS * IDX_DIM, P_IQ), col(ATT_WIDTH, P_AQ), col(2 * LANES, P_KV),
            pl.BlockSpec((Q_BLOCK, LANES), lambda b, j: (j, 0)),
            pl.BlockSpec((Q_BLOCK, LANES), lambda b, j: (j, 0)),
            pl.BlockSpec((1, LANES), lambda b, j: (0, 0)),
            pl.BlockSpec((1, LANES), lambda b, j: (0, 0)),
        ],
        out_specs=pl.BlockSpec((None, Q_BLOCK, ATT_WIDTH), lambda b, j: (b, j, 0)),
        scratch_shapes=[
            pltpu.VMEM((n_blocks, Q_BLOCK, LANES), BF16),
            pltpu.VMEM((n_blocks, Q_BLOCK, LANES), BF16),
            pltpu.VMEM((n_blocks, ATT_DIM, Q_BLOCK), BF16),
            pltpu.VMEM((IDX_HEADS // 2, seq, 2 * Q_BLOCK), F32),
            pltpu.VMEM((ATT_HEADS // 2, seq, 2 * Q_BLOCK), F32),
            pltpu.VMEM((seq, Q_BLOCK), F32),
            pltpu.VMEM((32, seq // 32, Q_BLOCK), jnp.int32),
            pltpu.VMEM((seq, ATT_HEADS * Q_BLOCK), BF16),
            pltpu.VMEM((1, Q_BLOCK), F32),
            pltpu.VMEM((2, Q_BLOCK), F32),
            pltpu.VMEM((1, Q_BLOCK), jnp.int32),
            pltpu.VMEM((8 * ATT_HEADS, Q_BLOCK), F32),
        ],
        compiler_params=_params(("arbitrary", "arbitrary")),
        name="dsa",
    )(proj, proj, proj, cos, sin, lng, lnb)


def _ssd_kernel(xbc_ref, z_ref, dt_ref, cw_ref, cb_ref, dtb_ref, alog_ref, dsk_ref, ng_ref, tril_ref,
                o_ref, xpad_ref, st_ref):
    @pl.when(pl.program_id(1) == 0)
    def _():
        xpad_ref[...] = jnp.zeros_like(xpad_ref)
        st_ref[...] = jnp.zeros_like(st_ref)

    for s in range(xbc_ref.shape[0]):
        _ssd_chunk(xbc_ref.at[s], z_ref.at[s], dt_ref.at[s], cw_ref, cb_ref, dtb_ref, alog_ref, dsk_ref, ng_ref,
                   tril_ref, o_ref.at[s], xpad_ref.at[s], st_ref.at[s])


def _ssd_chunk(xbc_ref, z_ref, dt_ref, cw_ref, cb_ref, dtb_ref, alog_ref, dsk_ref, ng_ref, tril_ref,
               o_ref, xpad_ref, st_ref):
    L = SSM_CHUNK
    HW = SSM_WIDTH // SSM_GROUPS

    x = xbc_ref[...]
    conv = cb_ref[...] + cw_ref[SSM_CONV - 1:SSM_CONV, :] * x
    for d in range(1, SSM_CONV):
        conv = conv + cw_ref[SSM_CONV - 1 - d:SSM_CONV - d, :] * pltpu.roll(x, d, 0)
    hx = jnp.concatenate([xpad_ref[...], x[0:8, :]], axis=0)
    head = cb_ref[...]
    for i in range(SSM_CONV):
        off = 8 - (SSM_CONV - 1) + i
        head = head + cw_ref[i:i + 1, :] * hx[off:off + 8, :]
    conv = jnp.concatenate([head, conv[8:, :]], axis=0)
    xpad_ref[...] = x[L - 8:L, :]
    xc = conv * _sigmoid(conv)
    xs = xc[:, 0:SSM_WIDTH]

    dtr = dt_ref[...] + dtb_ref[...]
    dt128 = jnp.maximum(dtr, 0.0) + jnp.log(1.0 + jnp.exp(-jnp.abs(dtr)))
    da128 = dt128 * (-jnp.exp(alog_ref[...]))
    acs128 = _cumsum_rows(tril_ref[...], da128)
    acs_t = acs128.T

    lane = lax.broadcasted_iota(jnp.int32, (L, LANES), 1)
    lo64 = lane < 64

    def col(x, h):
        return jnp.broadcast_to(x[:, h:h + 1], (L, LANES))

    def wide(x):
        return jnp.concatenate([jnp.where(lo64, col(x, 2 * m), col(x, 2 * m + 1)) for m in range(SSM_HEADS // 2)],
                               axis=1)

    dt_w = wide(dt128)
    acs_w = wide(acs128)
    aend_w = acs_w[L - 1:L, :]
    xdt = xs * dt_w
    xdt_dec = (xdt * jnp.exp(aend_w - acs_w)).astype(BF16)
    xdt_bf = xdt.astype(BF16)
    eacs_w = jnp.exp(acs_w)
    eaend_w = jnp.exp(aend_w)

    tri = lax.broadcasted_iota(jnp.int32, (L, L), 1) <= lax.broadcasted_iota(jnp.int32, (L, L), 0)

    def decay(h):
        diff = col(acs128, h) - jnp.broadcast_to(acs_t[h:h + 1, :], (L, L))
        return jnp.where(tri, jnp.exp(jnp.minimum(diff, 0.0)), 0.0)

    ys = []
    for g in range(SSM_GROUPS):
        bm = xc[:, SSM_WIDTH + g * SSM_STATE:SSM_WIDTH + (g + 1) * SSM_STATE]
        cm = xc[:, SSM_WIDTH + (SSM_GROUPS + g) * SSM_STATE:SSM_WIDTH + (SSM_GROUPS + g + 1) * SSM_STATE]
        cm_bf = cm.astype(BF16)
        cb = _dot_nt(cm_bf, bm.astype(BF16))
        st_g = st_ref[:, g * HW:(g + 1) * HW]
        y_off = _dot(cm_bf, st_g.astype(BF16)) * eacs_w[:, g * HW:(g + 1) * HW]
        new_st = _dot(bm.T.astype(BF16), xdt_dec[:, g * HW:(g + 1) * HW])
        st_ref[:, g * HW:(g + 1) * HW] = st_g * eaend_w[:, g * HW:(g + 1) * HW] + new_st
        slabs = []
        for mm in range(HW // LANES):
            m = g * (HW // LANES) + mm
            xslab = xdt_bf[:, m * LANES:(m + 1) * LANES]
            y0 = _dot((cb * decay(2 * m)).astype(BF16), xslab)
            y1 = _dot((cb * decay(2 * m + 1)).astype(BF16), xslab)
            slabs.append(jnp.where(lo64, y0, y1))
        ys.append(jnp.concatenate(slabs, axis=1) + y_off)
    y = jnp.concatenate(ys, axis=1) + xs * dsk_ref[...]
    z = z_ref[...]
    y = y * (z * _sigmoid(z))
    outs = []
    for g in range(SSM_GROUPS):
        yg = y[:, g * HW:(g + 1) * HW]
        yn = yg * lax.rsqrt(jnp.mean(yg * yg, axis=-1, keepdims=True) + EPS)
        outs.append(yn * ng_ref[:, g * HW:(g + 1) * HW])
    o_ref[...] = jnp.concatenate(outs, axis=1).astype(BF16)


def _ssd(proj, conv_w, conv_b, dt_bias, a_log, d_skip, norm_g):
    bsz, seq, _ = proj.shape
    pad = jnp.zeros((LANES - SSM_HEADS,), F32)
    dtb = jnp.concatenate([dt_bias, pad]).reshape(1, LANES)
    alog = jnp.concatenate([a_log, pad]).reshape(1, LANES)
    dsk = jnp.repeat(d_skip, SSM_HEADDIM).reshape(1, SSM_WIDTH)
    tril = jnp.tril(jnp.ones((SSM_CHUNK, SSM_CHUNK), BF16))
    col = lambda width, off: pl.BlockSpec((MIX_BATCH, SSM_CHUNK, width), lambda b, c: (b, c, off // width))
    full = lambda r, w: pl.BlockSpec((r, w), lambda b, c: (0, 0))
    return pl.pallas_call(
        _ssd_kernel,
        out_shape=jax.ShapeDtypeStruct((bsz, seq, SSM_WIDTH), BF16),
        grid=(bsz // MIX_BATCH, seq // SSM_CHUNK),
        in_specs=[
            col(CONV_CH, P_XBC), col(SSM_WIDTH, P_Z), col(LANES, P_DT),
            full(SSM_CONV, CONV_CH), full(1, CONV_CH), full(1, LANES), full(1, LANES),
            full(1, SSM_WIDTH), full(1, SSM_WIDTH), full(SSM_CHUNK, SSM_CHUNK),
        ],
        out_specs=pl.BlockSpec((MIX_BATCH, SSM_CHUNK, SSM_WIDTH), lambda b, c: (b, c, 0)),
        scratch_shapes=[
            pltpu.VMEM((MIX_BATCH, 8, CONV_CH), F32),
            pltpu.VMEM((MIX_BATCH, SSM_STATE, SSM_WIDTH), F32),
        ],
        compiler_params=_params(("arbitrary", "arbitrary")),
        name="ssd",
    )(proj, proj, proj, conv_w, conv_b.reshape(1, CONV_CH), dtb, alog, dsk, norm_g.reshape(1, SSM_WIDTH), tril)


def kernel(x, c, w_ada, b_ada, norm_g, w_ffn_gu, w_ffn_down, w_in, w_out, lb_logits, hg_norm_g, idx_k_norm_g,
           idx_k_norm_b, conv_w, conv_b, dt_bias, a_log, d_skip, ssm_norm_g, final_norm_g):
    bsz, seq, _ = x.shape
    mod4 = _ada(c, w_ada, b_ada).reshape(DEPTH, bsz, 9, D_MODEL)
    cos, sin = _rope_tables(seq)
    fg = final_norm_g.reshape(1, D_MODEL)
    for l in range(DEPTH):
        x = _ffn(x, mod4, norm_g, w_ffn_gu[l, 0].astype(BF16), w_ffn_down[l, 0].astype(BF16), fg, l, 0, False)
        proj = _inproj(x, mod4, norm_g, _pack_w_in(w_in[l]), l)
        o_a = _hgrn2(proj, lb_logits, hg_norm_g, l)
        o_b = _dsa(proj, cos, sin, idx_k_norm_g, idx_k_norm_b, l)
        o_c = _ssd(proj, conv_w[l], conv_b[l], dt_bias[l], a_log[l], d_skip[l], ssm_norm_g[l])
        x = _mix_ffn(x, o_a, o_b, o_c, w_out[l].astype(BF16), mod4, norm_g, w_ffn_gu[l, 1].astype(BF16),
                     w_ffn_down[l, 1].astype(BF16), fg, l, l == DEPTH - 1)
    return x
```
